```python
import jax, jax.numpy as jnp
from jax import lax
import numpy as np

D_MODEL = 2048
BATCH = 4
SEQ = 8192
DEPTH = 1

D_CONV = D_MODEL
CONV_WIDTH = 31
N_CONV_GROUPS = 16
RNN_BLOCKS = 16
RNN_BLOCK_DIM = 160
D_RNN = RNN_BLOCKS * RNN_BLOCK_DIM
RNN_CONV_WIDTH = 4
RG_C = 8.0
NORM_EPS = 1e-6
LN_EPS = 1e-5
IN_SIZES = (D_CONV, D_CONV, D_CONV, D_RNN, D_RNN, D_MODEL, D_MODEL)
D_IN = sum(IN_SIZES)

kernel_name = "hybrid_conformer_rglru_gated_block"


def _rmsnorm(x, g):
    xf = x.astype(jnp.float32)
    y = xf * lax.rsqrt(jnp.mean(xf * xf, axis=-1, keepdims=True) + NORM_EPS)
    return (y * g.astype(jnp.float32)).astype(x.dtype)


def _layernorm(x, g, b):
    xf = x.astype(jnp.float32)
    mu = jnp.mean(xf, axis=-1, keepdims=True)
    xc = xf - mu
    var = jnp.mean(xc * xc, axis=-1, keepdims=True)
    y = xc * lax.rsqrt(var + LN_EPS) * g.astype(jnp.float32) + b.astype(jnp.float32)
    return y.astype(x.dtype)


def _causal_depthwise_conv(x, w, b):
    width, ch = w.shape
    y = lax.conv_general_dilated(
        x, w[:, None, :].astype(x.dtype), window_strides=(1,), padding=[(width - 1, 0)],
        dimension_numbers=("NWC", "WIO", "NWC"), feature_group_count=ch)
    return y + b.astype(x.dtype)


def _block_diag(x, w, b):
    bsz, seq, _ = x.shape
    nh, dh, _ = w.shape
    y = jnp.einsum("bshi,hij->bshj", x.reshape(bsz, seq, nh, dh), w).reshape(bsz, seq, nh * dh)
    return y + b


def _rg_lru(x, w_a, b_a, w_x, b_x, lam):
    r = jax.nn.sigmoid(_block_diag(x, w_a, b_a).astype(jnp.float32))
    i = jax.nn.sigmoid(_block_diag(x, w_x, b_x).astype(jnp.float32))
    log_a = -RG_C * r * jax.nn.softplus(-lam.astype(jnp.float32))
    a = jnp.exp(log_a)
    mult = jnp.sqrt(jnp.maximum(-jnp.expm1(2.0 * log_a), 0.0))
    u = mult * (i * x.astype(jnp.float32))

    def combine(c1, c2):
        a1, b1 = c1
        a2, b2 = c2
        return a1 * a2, a2 * b1 + b2

    _, h = lax.associative_scan(combine, (a, u), axis=1)
    return h.astype(x.dtype)


def setup_inputs(seed: int = 0) -> dict:
    key = jax.random.key(seed)
    ks = jax.random.split(key, 20)
    f32 = jnp.float32
    L = DEPTH
    nrm = lambda k, shape, s: (jax.random.normal(k, shape, f32) * s)
    x = jax.random.normal(ks[0], (BATCH, SEQ, D_MODEL), f32)
    norm_g = 1.0 + nrm(ks[1], (L, D_MODEL), 0.02)
    w_in = nrm(ks[2], (L, D_MODEL, D_IN), D_MODEL ** -0.5)
    conv_dw_w = nrm(ks[3], (L, CONV_WIDTH, D_CONV), CONV_WIDTH ** -0.5)
    conv_dw_b = nrm(ks[4], (L, D_CONV), 0.02)
    conv_ln_g = 1.0 + nrm(ks[5], (L, D_CONV), 0.02)
    conv_ln_b = nrm(ks[6], (L, D_CONV), 0.02)
    w_conv_out = nrm(ks[7], (L, D_CONV, D_MODEL), D_CONV ** -0.5)
    rnn_conv_w = nrm(ks[8], (L, RNN_CONV_WIDTH, D_RNN), RNN_CONV_WIDTH ** -0.5)
    rnn_conv_b = nrm(ks[9], (L, D_RNN), 0.02)
    w_rg_a = nrm(ks[10], (L, RNN_BLOCKS, RNN_BLOCK_DIM, RNN_BLOCK_DIM), RNN_BLOCK_DIM ** -0.5)
    b_rg_a = nrm(ks[11], (L, D_RNN), 0.02)
    w_rg_x = nrm(ks[12], (L, RNN_BLOCKS, RNN_BLOCK_DIM, RNN_BLOCK_DIM), RNN_BLOCK_DIM ** -0.5)
    b_rg_x = nrm(ks[13], (L, D_RNN), 0.02)
    a_c = jax.random.uniform(ks[14], (L, D_RNN), f32, 0.9, 0.999)
    s = a_c ** (1.0 / RG_C)
    rg_lambda = jnp.log(s) - jnp.log1p(-s)
    w_rnn_out = nrm(ks[15], (L, D_RNN, D_MODEL), D_RNN ** -0.5)
    w_out = nrm(ks[16], (L, D_MODEL, D_MODEL), D_MODEL ** -0.5)
    final_norm_g = 1.0 + nrm(ks[17], (D_MODEL,), 0.02)
    return {"x": x, "norm_g": norm_g, "w_in": w_in,
            "conv_dw_w": conv_dw_w, "conv_dw_b": conv_dw_b, "conv_ln_g": conv_ln_g, "conv_ln_b": conv_ln_b,
            "w_conv_out": w_conv_out, "rnn_conv_w": rnn_conv_w, "rnn_conv_b": rnn_conv_b,
            "w_rg_a": w_rg_a, "b_rg_a": b_rg_a, "w_rg_x": w_rg_x, "b_rg_x": b_rg_x,
            "rg_lambda": rg_lambda, "w_rnn_out": w_rnn_out, "w_out": w_out, "final_norm_g": final_norm_g}


def reference(x, norm_g, w_in, conv_dw_w, conv_dw_b, conv_ln_g, conv_ln_b, w_conv_out,
              rnn_conv_w, rnn_conv_b, w_rg_a, b_rg_a, w_rg_x, b_rg_x, rg_lambda,
              w_rnn_out, w_out, final_norm_g):
    split_idx = [int(v) for v in np.cumsum(IN_SIZES)[:-1]]
    for l in range(DEPTH):
        h = _rmsnorm(x, norm_g[l])
        z = jnp.einsum("bsd,de->bse", h, w_in[l])
        c_val, c_glu, c_gate, r_x, r_gate, g_conv, g_rnn = jnp.split(z, split_idx, axis=-1)
        u = c_val * jax.nn.sigmoid(c_glu)
        u = _causal_depthwise_conv(u, conv_dw_w[l], conv_dw_b[l])
        u = jax.nn.silu(_layernorm(u, conv_ln_g[l], conv_ln_b[l]))
        y_conv = jnp.einsum("bsc,cd->bsd", u * jax.nn.silu(c_gate), w_conv_out[l])
        v = _causal_depthwise_conv(r_x, rnn_conv_w[l], rnn_conv_b[l])
        v = _rg_lru(v, w_rg_a[l], b_rg_a[l], w_rg_x[l], b_rg_x[l], rg_lambda[l])
        y_rnn = jnp.einsum("bsc,cd->bsd", v * jax.nn.silu(r_gate), w_rnn_out[l])
        merged = jax.nn.sigmoid(g_conv) * y_conv + jax.nn.sigmoid(g_rnn) * y_rnn
        x = x + jnp.einsum("bsd,de->bse", merged, w_out[l])
    return _rmsnorm(x, final_norm_g)
```

```python
import functools

import jax
import jax.numpy as jnp
from jax import lax
from jax.experimental import pallas as pl
from jax.experimental.pallas import tpu as pltpu

D_MODEL = 2048
D_CONV = 2048
CONV_WIDTH = 31
RNN_BLOCKS = 16
RNN_BLOCK_DIM = 160
D_RNN = RNN_BLOCKS * RNN_BLOCK_DIM
RNN_CONV_WIDTH = 4
RG_C = 8.0
NORM_EPS = 1e-6
LN_EPS = 1e-5

LANES = 128
SUBLANES = 8
VMEM_LIMIT = 56 * 1024 * 1024

TM = 256
CW = 512
CONV_HALO = 32
CONV_ROWS = 64
RNN_HALO = 8
RNN_GROUP = 4 * RNN_BLOCK_DIM
RG_WIN = 2 * LANES

_F32 = jnp.float32
_BF16 = jnp.bfloat16


def _dot(a, b):
    return jnp.dot(a, b, preferred_element_type=_F32)


def _sigmoid(x):
    return jax.nn.sigmoid(x)


def _silu(x):
    return x * jax.nn.sigmoid(x)


def _resident(shape):
    nd = len(shape)
    return pl.BlockSpec(shape, lambda *_: (0,) * nd, pipeline_mode=pl.Buffered(1))


def _conv_front_kernel(x_ref, ng_ref, wv_ref, wg_ref, cw_ref, cb_ref, h_ref, co_ref, ubuf):
    s = pl.program_id(1)

    @pl.when(s == 0)
    def _():
        ubuf[0:CONV_HALO, :] = jnp.zeros((CONV_HALO, D_CONV), _F32)
        ubuf[CONV_HALO + TM:, :] = jnp.zeros((SUBLANES, D_CONV), _F32)

    x = x_ref[0]
    ms = jnp.mean(x * x, axis=-1, keepdims=True)
    h = (x * lax.rsqrt(ms + NORM_EPS) * ng_ref[...]).astype(_BF16)
    h_ref[0] = h

    for j in range(D_CONV // CW):
        cols = slice(j * CW, (j + 1) * CW)
        val = _dot(h, wv_ref[:, cols])
        glu = _dot(h, wg_ref[:, cols])
        ubuf[CONV_HALO:CONV_HALO + TM, cols] = val * _sigmoid(glu)

    base = CONV_HALO - (CONV_WIDTH - 1)

    def conv_rows(i, carry):
        r0 = pl.multiple_of(i * CONV_ROWS, CONV_ROWS)
        for c in range(D_CONV // LANES):
            cols = slice(c * LANES, (c + 1) * LANES)
            win = ubuf[pl.ds(r0, CONV_ROWS + CONV_HALO + SUBLANES), cols]
            acc = jnp.broadcast_to(cb_ref[:, cols], (CONV_ROWS, LANES))
            for r in range(SUBLANES):
                shifted = win[r:r + CONV_ROWS + CONV_HALO]
                for k in range(CONV_WIDTH):
                    shift = k + base
                    if shift % SUBLANES != r:
                        continue
                    q = shift - r
                    acc = acc + shifted[q:q + CONV_ROWS] * cw_ref[k:k + 1, cols]
            co_ref[0, pl.ds(r0, CONV_ROWS), cols] = acc
        return carry

    lax.fori_loop(0, TM // CONV_ROWS, conv_rows, 0)
    ubuf[0:CONV_HALO, :] = ubuf[TM:TM + CONV_HALO, :]


def _conv_front(x, norm_g, w_val, w_glu, conv_w, conv_b):
    bsz, seq, _ = x.shape
    return pl.pallas_call(
        _conv_front_kernel,
        name="conv_front",
        grid=(bsz, seq // TM),
        in_specs=[
            pl.BlockSpec((1, TM, D_MODEL), lambda b, s: (b, s, 0)),
            _resident((1, D_MODEL)),
            _resident((D_MODEL, D_CONV)),
            _resident((D_MODEL, D_CONV)),
            _resident((CONV_WIDTH, D_CONV)),
            _resident((1, D_CONV)),
        ],
        out_specs=[
            pl.BlockSpec((1, TM, D_MODEL), lambda b, s: (b, s, 0)),
            pl.BlockSpec((1, TM, D_CONV), lambda b, s: (b, s, 0)),
        ],
        out_shape=[
            jax.ShapeDtypeStruct((bsz, seq, D_MODEL), _BF16),
            jax.ShapeDtypeStruct((bsz, seq, D_CONV), _F32),
        ],
        scratch_shapes=[pltpu.VMEM((CONV_HALO + TM + SUBLANES, D_CONV), _F32)],
        compiler_params=pltpu.CompilerParams(
            dimension_semantics=("arbitrary", "arbitrary"), vmem_limit_bytes=VMEM_LIMIT),
    )(x, norm_g, w_val, w_glu, conv_w, conv_b)


def _conv_back_kernel(co_ref, h_ref, lg_ref, lb_ref, wcg_ref, wco_ref, wgc_ref, mc_ref, y_s, a_s):
    co = co_ref[...]
    mu = jnp.mean(co, axis=-1, keepdims=True)
    xc = co - mu
    var = jnp.mean(xc * xc, axis=-1, keepdims=True)
    y = xc * lax.rsqrt(var + LN_EPS) * lg_ref[...] + lb_ref[...]
    y_s[...] = _silu(y)
    h = h_ref[...]
    for j in range(D_CONV // CW):
        cols = slice(j * CW, (j + 1) * CW)
        gate = _dot(h, wcg_ref[:, cols])
        a_s[:, cols] = (y_s[:, cols] * _silu(gate)).astype(_BF16)
    a = a_s[...]
    for j in range(D_MODEL // CW):
        cols = slice(j * CW, (j + 1) * CW)
        yc = _dot(a, wco_ref[:, cols])
        gc = _dot(h, wgc_ref[:, cols])
        mc_ref[:, cols] = _sigmoid(gc) * yc


def _conv_back(co, h, ln_g, ln_b, w_cgate, w_conv_out, w_gconv):
    m = co.shape[0]
    return pl.pallas_call(
        _conv_back_kernel,
        name="conv_back",
        grid=(m // TM,),
        in_specs=[
            pl.BlockSpec((TM, D_CONV), lambda i: (i, 0)),
            pl.BlockSpec((TM, D_MODEL), lambda i: (i, 0)),
            _resident((1, D_CONV)),
            _resident((1, D_CONV)),
            _resident((D_MODEL, D_CONV)),
            _resident((D_CONV, D_MODEL)),
            _resident((D_MODEL, D_MODEL)),
        ],
        out_specs=pl.BlockSpec((TM, D_MODEL), lambda i: (i, 0)),
        out_shape=jax.ShapeDtypeStruct((m, D_MODEL), _F32),
        scratch_shapes=[pltpu.VMEM((TM, D_CONV), _F32), pltpu.VMEM((TM, D_CONV), _BF16)],
        compiler_params=pltpu.CompilerParams(
            dimension_semantics=("arbitrary",), vmem_limit_bytes=VMEM_LIMIT),
    )(co, h, ln_g, ln_b, w_cgate, w_conv_out, w_gconv)


def _rnn_front_kernel(h_ref, wrx_ref, wrg_ref, rcw_ref, rcb_ref, wblk_ref, ba_ref, bx_ref,
                      lam_ref, o_ref, rbuf, a_s, u_s, carry_s):
    s = pl.program_id(1)

    @pl.when(s == 0)
    def _():
        rbuf[0:RNN_HALO, :] = jnp.zeros((RNN_HALO, D_RNN), _F32)
        rbuf[RNN_HALO + TM:, :] = jnp.zeros((SUBLANES, D_RNN), _F32)
        carry_s[...] = jnp.zeros((SUBLANES, D_RNN), _F32)

    h = h_ref[0]
    base = RNN_HALO - (RNN_CONV_WIDTH - 1)
    row_id = lax.broadcasted_iota(jnp.int32, (SUBLANES, RNN_GROUP), 0)

    for g in range(D_RNN // RNN_GROUP):
        gcols = slice(g * RNN_GROUP, (g + 1) * RNN_GROUP)
        rbuf[RNN_HALO:RNN_HALO + TM, gcols] = _dot(h, wrx_ref[:, gcols])

        win = rbuf[:, gcols]
        v = jnp.broadcast_to(rcb_ref[:, gcols], (TM, RNN_GROUP))
        for k in range(RNN_CONV_WIDTH):
            shift = k + base
            v = v + win[shift:shift + TM] * rcw_ref[k:k + 1, gcols]
        vb = v.astype(_BF16)

        pieces_a = [None] * 5
        pieces_x = [None] * 5
        for w in range(4):
            p = _dot(vb[:, w * LANES:w * LANES + RG_WIN], wblk_ref[4 * g + w])
            for half in range(2):
                pa = p[:, half * LANES:(half + 1) * LANES]
                px = p[:, RG_WIN + half * LANES:RG_WIN + (half + 1) * LANES]
                idx = w + half
                pieces_a[idx] = pa if pieces_a[idx] is None else pieces_a[idx] + pa
                pieces_x[idx] = px if pieces_x[idx] is None else pieces_x[idx] + px
        pre_a = jnp.concatenate(pieces_a, axis=1) + ba_ref[:, gcols]
        pre_x = jnp.concatenate(pieces_x, axis=1) + bx_ref[:, gcols]

        r_gate = _sigmoid(pre_a)
        i_gate = _sigmoid(pre_x)
        neg_lam = -lam_ref[:, gcols]
        softplus = jnp.maximum(neg_lam, 0.0) + jnp.log1p(jnp.exp(-jnp.abs(neg_lam)))
        log_a = r_gate * (-RG_C * softplus)
        a = jnp.exp(log_a)
        mult = jnp.sqrt(jnp.maximum(jnp.tanh(-log_a) * (1.0 + a * a), 0.0))
        a_s[...] = a
        u_s[...] = mult * (i_gate * v)

        def scan_rows(i, hprev):
            r0 = pl.multiple_of(i * SUBLANES, SUBLANES)
            aa = a_s[pl.ds(r0, SUBLANES), :]
            uu = u_s[pl.ds(r0, SUBLANES), :]
            for d in (1, 2, 4):
                keep = row_id >= d
                u_sh = jnp.where(keep, pltpu.roll(uu, d, axis=0), 0.0)
                a_sh = jnp.where(keep, pltpu.roll(aa, d, axis=0), 1.0)
                uu = uu + aa * u_sh
                aa = aa * a_sh
            hh = uu + aa * hprev
            u_s[pl.ds(r0, SUBLANES), :] = hh
            return jnp.broadcast_to(hh[SUBLANES - 1:SUBLANES, :], (SUBLANES, RNN_GROUP))

        hlast = lax.fori_loop(0, TM // SUBLANES, scan_rows, carry_s[:, gcols])
        carry_s[:, gcols] = hlast

        gate = _dot(h, wrg_ref[:, gcols])
        o_ref[0, :, gcols] = (u_s[...] * _silu(gate)).astype(_BF16)

    rbuf[0:RNN_HALO, :] = rbuf[TM:TM + RNN_HALO, :]


def _rnn_front(h, w_rx, w_rgate, rnn_conv_w, rnn_conv_b, w_blk, b_a, b_x, lam):
    bsz, seq, _ = h.shape
    return pl.pallas_call(
        _rnn_front_kernel,
        name="rnn_front",
        grid=(bsz, seq // TM),
        in_specs=[
            pl.BlockSpec((1, TM, D_MODEL), lambda b, s: (b, s, 0)),
            _resident((D_MODEL, D_RNN)),
            _resident((D_MODEL, D_RNN)),
            _resident((RNN_CONV_WIDTH, D_RNN)),
            _resident((1, D_RNN)),
            _resident((RNN_BLOCKS, RG_WIN, 2 * RG_WIN)),
            _resident((1, D_RNN)),
            _resident((1, D_RNN)),
            _resident((1, D_RNN)),
        ],
        out_specs=pl.BlockSpec((1, TM, D_RNN), lambda b, s: (b, s, 0)),
        out_shape=jax.ShapeDtypeStruct((bsz, seq, D_RNN), _BF16),
        scratch_shapes=[
            pltpu.VMEM((RNN_HALO + TM + SUBLANES, D_RNN), _F32),
            pltpu.VMEM((TM, RNN_GROUP), _F32),
            pltpu.VMEM((TM, RNN_GROUP), _F32),
            pltpu.VMEM((SUBLANES, D_RNN), _F32),
        ],
        compiler_params=pltpu.CompilerParams(
            dimension_semantics=("arbitrary", "arbitrary"), vmem_limit_bytes=VMEM_LIMIT),
    )(h, w_rx, w_rgate, rnn_conv_w, rnn_conv_b, w_blk, b_a, b_x, lam)


def _merge_out_kernel(o_ref, h_ref, mc_ref, x_ref, wro_ref, wgr_ref, wout_ref, fg_ref,
                      out_ref, m_s, res_s):
    o = o_ref[...]
    h = h_ref[...]
    for j in range(D_MODEL // CW):
        cols = slice(j * CW, (j + 1) * CW)
        yr = _dot(o, wro_ref[:, cols])
        gr = _dot(h, wgr_ref[:, cols])
        m_s[:, cols] = (mc_ref[:, cols] + _sigmoid(gr) * yr).astype(_BF16)
    merged = m_s[...]
    for j in range(D_MODEL // CW):
        cols = slice(j * CW, (j + 1) * CW)
        res_s[:, cols] = x_ref[:, cols] + _dot(merged, wout_ref[:, cols])
    res = res_s[...]
    ms = jnp.mean(res * res, axis=-1, keepdims=True)
    out_ref[...] = res * lax.rsqrt(ms + NORM_EPS) * fg_ref[...]


def _merge_out(o, h, mc, x, w_rnn_out, w_grnn, w_out, final_g):
    m = x.shape[0]
    return pl.pallas_call(
        _merge_out_kernel,
        name="merge_out",
        grid=(m // TM,),
        in_specs=[
            pl.BlockSpec((TM, D_RNN), lambda i: (i, 0)),
            pl.BlockSpec((TM, D_MODEL), lambda i: (i, 0)),
            pl.BlockSpec((TM, D_MODEL), lambda i: (i, 0)),
            pl.BlockSpec((TM, D_MODEL), lambda i: (i, 0)),
            _resident((D_RNN, D_MODEL)),
            _resident((D_MODEL, D_MODEL)),
            _resident((D_MODEL, D_MODEL)),
            _resident((1, D_MODEL)),
        ],
        out_specs=pl.BlockSpec((TM, D_MODEL), lambda i: (i, 0)),
        out_shape=jax.ShapeDtypeStruct((m, D_MODEL), _F32),
        scratch_shapes=[pltpu.VMEM((TM, D_MODEL), _BF16), pltpu.VMEM((TM, D_MODEL), _F32)],
        compiler_params=pltpu.CompilerParams(
            dimension_semantics=("arbitrary",), vmem_limit_bytes=VMEM_LIMIT),
    )(o, h, mc, x, w_rnn_out, w_grnn, w_out, final_g)


def _window_gate_weights(w_a, w_x):
    outs = []
    for blk in range(RNN_BLOCKS):
        off = (blk * RNN_BLOCK_DIM) % LANES
        pad = ((off, RG_WIN - RNN_BLOCK_DIM - off),) * 2
        outs.append(jnp.concatenate([jnp.pad(w_a[blk], pad), jnp.pad(w_x[blk], pad)], axis=1))
    return jnp.stack(outs).astype(_BF16)


def _layer(x, norm_g, w_in, conv_dw_w, conv_dw_b, conv_ln_g, conv_ln_b, w_conv_out,
           rnn_conv_w, rnn_conv_b, w_rg_a, b_rg_a, w_rg_x, b_rg_x, rg_lambda,
           w_rnn_out, w_out, out_g):
    bsz, seq, _ = x.shape
    m = bsz * seq
    row = lambda v: v.reshape(1, -1).astype(_F32)
    bounds = [0]
    for width in (D_CONV, D_CONV, D_CONV, D_RNN, D_RNN, D_MODEL, D_MODEL):
        bounds.append(bounds[-1] + width)
    w_val, w_glu, w_cgate, w_rx, w_rgate, w_gconv, w_grnn = [
        w_in[:, bounds[i]:bounds[i + 1]].astype(_BF16) for i in range(7)]

    h, co = _conv_front(x, row(norm_g), w_val, w_glu, conv_dw_w.astype(_F32), row(conv_dw_b))
    mc = _conv_back(co.reshape(m, D_CONV), h.reshape(m, D_MODEL), row(conv_ln_g), row(conv_ln_b),
                    w_cgate, w_conv_out.astype(_BF16), w_gconv)
    o = _rnn_front(h, w_rx, w_rgate, rnn_conv_w.astype(_F32), row(rnn_conv_b),
                   _window_gate_weights(w_rg_a, w_rg_x), row(b_rg_a), row(b_rg_x), row(rg_lambda))
    out = _merge_out(o.reshape(m, D_RNN), h.reshape(m, D_MODEL), mc, x.reshape(m, D_MODEL),
                     w_rnn_out.astype(_BF16), w_grnn, w_out.astype(_BF16), row(out_g))
    return out.reshape(bsz, seq, D_MODEL)


def kernel(x, norm_g, w_in, conv_dw_w, conv_dw_b, conv_ln_g, conv_ln_b, w_conv_out, rnn_conv_w,
           rnn_conv_b, w_rg_a, b_rg_a, w_rg_x, b_rg_x, rg_lambda, w_rnn_out, w_out, final_norm_g):
    depth = norm_g.shape[0]
    assert depth == 1, "the final RMSNorm is fused into the single layer's last call"
    return _layer(x, norm_g[0], w_in[0], conv_dw_w[0], conv_dw_b[0], conv_ln_g[0], conv_ln_b[0],
                  w_conv_out[0], rnn_conv_w[0], rnn_conv_b[0], w_rg_a[0], b_rg_a[0], w_rg_x[0],
                  b_rg_x[0], rg_lambda[0], w_rnn_out[0], w_out[0], final_norm_g)
```

```python
import functools

import jax
import jax.numpy as jnp
from jax import lax
from jax.experimental import pallas as pl
from jax.experimental.pallas import tpu as pltpu

D_MODEL = 2048
D_CONV = 2048
CONV_WIDTH = 31
RNN_BLOCKS = 16
RNN_BLOCK_DIM = 160
D_RNN = RNN_BLOCKS * RNN_BLOCK_DIM
RNN_CONV_WIDTH = 4
RG_C = 8.0
NORM_EPS = 1e-6
LN_EPS = 1e-5

LANES = 128
SUBLANES = 8
VMEM_LIMIT = 56 * 1024 * 1024

TM = 256
CW = 512
PW = 256
CONV_HALO = 32
CONV_ROWS = 64
SH_ROWS = TM + CONV_HALO - SUBLANES
RNN_HALO = 8
RNN_GROUP = 4 * RNN_BLOCK_DIM
RG_WIN = 2 * LANES

_F32 = jnp.float32
_BF16 = jnp.bfloat16


def _dot(a, b):
    return jnp.dot(a, b, preferred_element_type=_F32)


def _sigmoid(x):
    return jax.nn.sigmoid(x)


def _silu(x):
    return x * jax.nn.sigmoid(x)


def _resident(shape):
    nd = len(shape)
    return pl.BlockSpec(shape, lambda *_: (0,) * nd, pipeline_mode=pl.Buffered(1))


def _conv_front_kernel(x_ref, ng_ref, wv_ref, wg_ref, cw_ref, cb_ref, h_ref, co_ref, sh):
    s = pl.program_id(1)
    n_pieces = D_CONV // PW

    @pl.when(s == 0)
    def _():
        sh[:, 0, 0:CONV_HALO, :] = jnp.zeros((n_pieces, CONV_HALO, PW), _F32)

    x = x_ref[0]
    ms = jnp.mean(x * x, axis=-1, keepdims=True)
    h = (x * lax.rsqrt(ms + NORM_EPS) * ng_ref[...]).astype(_BF16)
    h_ref[0] = h

    base = CONV_HALO - (CONV_WIDTH - 1)

    for p in range(n_pieces):
        cols = slice(p * PW, (p + 1) * PW)
        val = _dot(h, wv_ref[:, cols])
        glu = _dot(h, wg_ref[:, cols])
        sh[p, 0, CONV_HALO:CONV_HALO + TM, :] = val * _sigmoid(glu)
        for r in range(1, SUBLANES):
            sh[p, r, 0:SH_ROWS, :] = sh[p, 0, r:r + SH_ROWS, :]
        for half in range(PW // LANES):
            lane = slice(half * LANES, (half + 1) * LANES)
            out_lane = slice(p * PW + half * LANES, p * PW + (half + 1) * LANES)
            for r0 in range(0, TM, CONV_ROWS):
                acc = jnp.broadcast_to(cb_ref[:, out_lane], (CONV_ROWS, LANES))
                for k in range(CONV_WIDTH):
                    r = (k + base) % SUBLANES
                    q = r0 + k + base - r
                    acc = acc + sh[p, r, q:q + CONV_ROWS, lane] * cw_ref[k:k + 1, out_lane]
                co_ref[0, r0:r0 + CONV_ROWS, out_lane] = acc

    sh[:, 0, 0:CONV_HALO, :] = sh[:, 0, TM:TM + CONV_HALO, :]


def _conv_front(x, norm_g, w_val, w_glu, conv_w, conv_b):
    bsz, seq, _ = x.shape
    return pl.pallas_call(
        _conv_front_kernel,
        name="conv_front",
        grid=(bsz, seq // TM),
        in_specs=[
            pl.BlockSpec((1, TM, D_MODEL), lambda b, s: (b, s, 0)),
            _resident((1, D_MODEL)),
            _resident((D_MODEL, D_CONV)),
            _resident((D_MODEL, D_CONV)),
            _resident((CONV_WIDTH, D_CONV)),
            _resident((1, D_CONV)),
        ],
        out_specs=[
            pl.BlockSpec((1, TM, D_MODEL), lambda b, s: (b, s, 0)),
            pl.BlockSpec((1, TM, D_CONV), lambda b, s: (b, s, 0)),
        ],
        out_shape=[
            jax.ShapeDtypeStruct((bsz, seq, D_MODEL), _BF16),
            jax.ShapeDtypeStruct((bsz, seq, D_CONV), _F32),
        ],
        scratch_shapes=[pltpu.VMEM((D_CONV // PW, SUBLANES, CONV_HALO + TM, PW), _F32)],
        compiler_params=pltpu.CompilerParams(
            dimension_semantics=("arbitrary", "arbitrary"), vmem_limit_bytes=VMEM_LIMIT),
    )(x, norm_g, w_val, w_glu, conv_w, conv_b)


def _conv_back_kernel(co_ref, h_ref, lg_ref, lb_ref, wcg_ref, wco_ref, wgc_ref, mc_ref, y_s, a_s):
    co = co_ref[...]
    mu = jnp.mean(co, axis=-1, keepdims=True)
    xc = co - mu
    var = jnp.mean(xc * xc, axis=-1, keepdims=True)
    y = xc * lax.rsqrt(var + LN_EPS) * lg_ref[...] + lb_ref[...]
    y_s[...] = _silu(y)
    h = h_ref[...]
    for j in range(D_CONV // CW):
        cols = slice(j * CW, (j + 1) * CW)
        gate = _dot(h, wcg_ref[:, cols])
        a_s[:, cols] = (y_s[:, cols] * _silu(gate)).astype(_BF16)
    a = a_s[...]
    for j in range(D_MODEL // CW):
        cols = slice(j * CW, (j + 1) * CW)
        yc = _dot(a, wco_ref[:, cols])
        gc = _dot(h, wgc_ref[:, cols])
        mc_ref[:, cols] = _sigmoid(gc) * yc


def _conv_back(co, h, ln_g, ln_b, w_cgate, w_conv_out, w_gconv):
    m = co.shape[0]
    return pl.pallas_call(
        _conv_back_kernel,
        name="conv_back",
        grid=(m // TM,),
        in_specs=[
            pl.BlockSpec((TM, D_CONV), lambda i: (i, 0)),
            pl.BlockSpec((TM, D_MODEL), lambda i: (i, 0)),
            _resident((1, D_CONV)),
            _resident((1, D_CONV)),
            _resident((D_MODEL, D_CONV)),
            _resident((D_CONV, D_MODEL)),
            _resident((D_MODEL, D_MODEL)),
        ],
        out_specs=pl.BlockSpec((TM, D_MODEL), lambda i: (i, 0)),
        out_shape=jax.ShapeDtypeStruct((m, D_MODEL), _F32),
        scratch_shapes=[pltpu.VMEM((TM, D_CONV), _F32), pltpu.VMEM((TM, D_CONV), _BF16)],
        compiler_params=pltpu.CompilerParams(
            dimension_semantics=("arbitrary",), vmem_limit_bytes=VMEM_LIMIT),
    )(co, h, ln_g, ln_b, w_cgate, w_conv_out, w_gconv)


def _rnn_front_kernel(h_ref, wrx_ref, wrg_ref, rcw_ref, rcb_ref, wblk_ref, ba_ref, bx_ref,
                      lam_ref, o_ref, rbuf, a_s, u_s, carry_s):
    s = pl.program_id(1)

    @pl.when(s == 0)
    def _():
        rbuf[0:RNN_HALO, :] = jnp.zeros((RNN_HALO, D_RNN), _F32)
        carry_s[...] = jnp.zeros((SUBLANES, D_RNN), _F32)

    h = h_ref[0]
    row_id = lax.broadcasted_iota(jnp.int32, (SUBLANES, RNN_GROUP), 0)
    base = RNN_HALO - (RNN_CONV_WIDTH - 1)

    for g in range(D_RNN // RNN_GROUP):
        gcols = slice(g * RNN_GROUP, (g + 1) * RNN_GROUP)
        rbuf[RNN_HALO:RNN_HALO + TM, gcols] = _dot(h, wrx_ref[:, gcols])

        v = jnp.broadcast_to(rcb_ref[:, gcols], (TM, RNN_GROUP))
        for k in range(RNN_CONV_WIDTH):
            v = v + rbuf[base + k:base + k + TM, gcols] * rcw_ref[k:k + 1, gcols]
        vb = v.astype(_BF16)

        pieces_a = [None] * 5
        pieces_x = [None] * 5
        for w in range(4):
            p = _dot(vb[:, w * LANES:w * LANES + RG_WIN], wblk_ref[4 * g + w])
            for half in range(2):
                pa = p[:, half * LANES:(half + 1) * LANES]
                px = p[:, RG_WIN + half * LANES:RG_WIN + (half + 1) * LANES]
                idx = w + half
                pieces_a[idx] = pa if pieces_a[idx] is None else pieces_a[idx] + pa
                pieces_x[idx] = px if pieces_x[idx] is None else pieces_x[idx] + px
        pre_a = jnp.concatenate(pieces_a, axis=1) + ba_ref[:, gcols]
        pre_x = jnp.concatenate(pieces_x, axis=1) + bx_ref[:, gcols]

        r_gate = _sigmoid(pre_a)
        i_gate = _sigmoid(pre_x)
        neg_lam = -lam_ref[:, gcols]
        softplus = jnp.maximum(neg_lam, 0.0) + jnp.log1p(jnp.exp(-jnp.abs(neg_lam)))
        log_a = r_gate * (-RG_C * softplus)
        a = jnp.exp(log_a)
        mult = jnp.sqrt(jnp.maximum(jnp.tanh(-log_a) * (1.0 + a * a), 0.0))
        a_s[:, gcols] = a
        u_s[:, gcols] = mult * (i_gate * v)

        hprev = carry_s[:, gcols]
        for r0 in range(0, TM, SUBLANES):
            aa = a_s[r0:r0 + SUBLANES, gcols]
            uu = u_s[r0:r0 + SUBLANES, gcols]
            for d in (1, 2, 4):
                keep = row_id >= d
                u_sh = jnp.where(keep, pltpu.roll(uu, d, axis=0), 0.0)
                a_sh = jnp.where(keep, pltpu.roll(aa, d, axis=0), 1.0)
                uu = uu + aa * u_sh
                aa = aa * a_sh
            hh = uu + aa * hprev
            u_s[r0:r0 + SUBLANES, gcols] = hh
            hprev = jnp.broadcast_to(hh[SUBLANES - 1:SUBLANES, :], (SUBLANES, RNN_GROUP))
        carry_s[:, gcols] = hprev

        gate = _dot(h, wrg_ref[:, gcols])
        o_ref[0, :, gcols] = (u_s[:, gcols] * _silu(gate)).astype(_BF16)

    rbuf[0:RNN_HALO, :] = rbuf[TM:TM + RNN_HALO, :]


def _rnn_front(h, w_rx, w_rgate, rnn_conv_w, rnn_conv_b, w_blk, b_a, b_x, lam):
    bsz, seq, _ = h.shape
    return pl.pallas_call(
        _rnn_front_kernel,
        name="rnn_front",
        grid=(bsz, seq // TM),
        in_specs=[
            pl.BlockSpec((1, TM, D_MODEL), lambda b, s: (b, s, 0)),
            _resident((D_MODEL, D_RNN)),
            _resident((D_MODEL, D_RNN)),
            _resident((RNN_CONV_WIDTH, D_RNN)),
            _resident((1, D_RNN)),
            _resident((RNN_BLOCKS, RG_WIN, 2 * RG_WIN)),
            _resident((1, D_RNN)),
            _resident((1, D_RNN)),
            _resident((1, D_RNN)),
        ],
        out_specs=pl.BlockSpec((1, TM, D_RNN), lambda b, s: (b, s, 0)),
        out_shape=jax.ShapeDtypeStruct((bsz, seq, D_RNN), _BF16),
        scratch_shapes=[
            pltpu.VMEM((RNN_HALO + TM, D_RNN), _F32),
            pltpu.VMEM((TM, D_RNN), _F32),
            pltpu.VMEM((TM, D_RNN), _F32),
            pltpu.VMEM((SUBLANES, D_RNN), _F32),
        ],
        compiler_params=pltpu.CompilerParams(
            dimension_semantics=("arbitrary", "arbitrary"), vmem_limit_bytes=VMEM_LIMIT),
    )(h, w_rx, w_rgate, rnn_conv_w, rnn_conv_b, w_blk, b_a, b_x, lam)


def _merge_out_kernel(o_ref, h_ref, mc_ref, x_ref, wro_ref, wgr_ref, wout_ref, fg_ref,
                      out_ref, m_s, res_s):
    o = o_ref[...]
    h = h_ref[...]
    for j in range(D_MODEL // CW):
        cols = slice(j * CW, (j + 1) * CW)
        yr = _dot(o, wro_ref[:, cols])
        gr = _dot(h, wgr_ref[:, cols])
        m_s[:, cols] = (mc_ref[:, cols] + _sigmoid(gr) * yr).astype(_BF16)
    merged = m_s[...]
    for j in range(D_MODEL // CW):
        cols = slice(j * CW, (j + 1) * CW)
        res_s[:, cols] = x_ref[:, cols] + _dot(merged, wout_ref[:, cols])
    res = res_s[...]
    ms = jnp.mean(res * res, axis=-1, keepdims=True)
    out_ref[...] = res * lax.rsqrt(ms + NORM_EPS) * fg_ref[...]


def _merge_out(o, h, mc, x, w_rnn_out, w_grnn, w_out, final_g):
    m = x.shape[0]
    return pl.pallas_call(
        _merge_out_kernel,
        name="merge_out",
        grid=(m // TM,),
        in_specs=[
            pl.BlockSpec((TM, D_RNN), lambda i: (i, 0)),
            pl.BlockSpec((TM, D_MODEL), lambda i: (i, 0)),
            pl.BlockSpec((TM, D_MODEL), lambda i: (i, 0)),
            pl.BlockSpec((TM, D_MODEL), lambda i: (i, 0)),
            _resident((D_RNN, D_MODEL)),
            _resident((D_MODEL, D_MODEL)),
            _resident((D_MODEL, D_MODEL)),
            _resident((1, D_MODEL)),
        ],
        out_specs=pl.BlockSpec((TM, D_MODEL), lambda i: (i, 0)),
        out_shape=jax.ShapeDtypeStruct((m, D_MODEL), _F32),
        scratch_shapes=[pltpu.VMEM((TM, D_MODEL), _BF16), pltpu.VMEM((TM, D_MODEL), _F32)],
        compiler_params=pltpu.CompilerParams(
            dimension_semantics=("arbitrary",), vmem_limit_bytes=VMEM_LIMIT),
    )(o, h, mc, x, w_rnn_out, w_grnn, w_out, final_g)


def _window_gate_weights(w_a, w_x):
    outs = []
    for blk in range(RNN_BLOCKS):
        off = (blk * RNN_BLOCK_DIM) % LANES
        pad = ((off, RG_WIN - RNN_BLOCK_DIM - off),) * 2
        outs.append(jnp.concatenate([jnp.pad(w_a[blk], pad), jnp.pad(w_x[blk], pad)], axis=1))
    return jnp.stack(outs).astype(_BF16)


def _layer(x, norm_g, w_in, conv_dw_w, conv_dw_b, conv_ln_g, conv_ln_b, w_conv_out,
           rnn_conv_w, rnn_conv_b, w_rg_a, b_rg_a, w_rg_x, b_rg_x, rg_lambda,
           w_rnn_out, w_out, out_g):
    bsz, seq, _ = x.shape
    m = bsz * seq
    row = lambda v: v.reshape(1, -1).astype(_F32)
    bounds = [0]
    for width in (D_CONV, D_CONV, D_CONV, D_RNN, D_RNN, D_MODEL, D_MODEL):
        bounds.append(bounds[-1] + width)
    w_val, w_glu, w_cgate, w_rx, w_rgate, w_gconv, w_grnn = [
        w_in[:, bounds[i]:bounds[i + 1]].astype(_BF16) for i in range(7)]

    h, co = _conv_front(x, row(norm_g), w_val, w_glu, conv_dw_w.astype(_F32), row(conv_dw_b))
    mc = _conv_back(co.reshape(m, D_CONV), h.reshape(m, D_MODEL), row(conv_ln_g), row(conv_ln_b),
                    w_cgate, w_conv_out.astype(_BF16), w_gconv)
    o = _rnn_front(h, w_rx, w_rgate, rnn_conv_w.astype(_F32), row(rnn_conv_b),
                   _window_gate_weights(w_rg_a, w_rg_x), row(b_rg_a), row(b_rg_x), row(rg_lambda))
    out = _merge_out(o.reshape(m, D_RNN), h.reshape(m, D_MODEL), mc, x.reshape(m, D_MODEL),
                     w_rnn_out.astype(_BF16), w_grnn, w_out.astype(_BF16), row(out_g))
    return out.reshape(bsz, seq, D_MODEL)


def kernel(x, norm_g, w_in, conv_dw_w, conv_dw_b, conv_ln_g, conv_ln_b, w_conv_out, rnn_conv_w,
           rnn_conv_b, w_rg_a, b_rg_a, w_rg_x, b_rg_x, rg_lambda, w_rnn_out, w_out, final_norm_g):
    depth = norm_g.shape[0]
    assert depth == 1, "the final RMSNorm is fused into the single layer's last call"
    return _layer(x, norm_g[0], w_in[0], conv_dw_w[0], conv_dw_b[0], conv_ln_g[0], conv_ln_b[0],
                  w_conv_out[0], rnn_conv_w[0], rnn_conv_b[0], w_rg_a[0], b_rg_a[0], w_rg_x[0],
                  b_rg_x[0], rg_lambda[0], w_rnn_out[0], w_out[0], final_norm_g)
```

```python
import functools

import jax
import jax.numpy as jnp
from jax import lax
from jax.experimental import pallas as pl
from jax.experimental.pallas import tpu as pltpu

D_MODEL = 2048
D_CONV = 2048
CONV_WIDTH = 31
RNN_BLOCKS = 16
RNN_BLOCK_DIM = 160
D_RNN = RNN_BLOCKS * RNN_BLOCK_DIM
RNN_CONV_WIDTH = 4
RG_C = 8.0
NORM_EPS = 1e-6
LN_EPS = 1e-5

LANES = 128
SUBLANES = 8
VMEM_LIMIT = 56 * 1024 * 1024

TM = 256
CW = 512
PW = 256
CONV_HALO = 32
CONV_ROWS = 64
SH_ROWS = TM + CONV_HALO - SUBLANES
RNN_HALO = 8
RNN_GROUP = 4 * RNN_BLOCK_DIM
RG_WIN = 2 * LANES

_F32 = jnp.float32
_BF16 = jnp.bfloat16


def _dot(a, b):
    return jnp.dot(a, b, preferred_element_type=_F32)


def _sigmoid(x):
    return jax.nn.sigmoid(x)


def _silu(x):
    return x * jax.nn.sigmoid(x)


def _resident(shape):
    nd = len(shape)
    return pl.BlockSpec(shape, lambda *_: (0,) * nd, pipeline_mode=pl.Buffered(1))


def _conv_front_kernel(x_ref, ng_ref, wv_ref, wg_ref, cw_ref, cb_ref, h_ref, co_ref, sh, halo):
    s = pl.program_id(1)
    n_pieces = D_CONV // PW

    @pl.when(s == 0)
    def _():
        halo[...] = jnp.zeros(halo.shape, _F32)

    x = x_ref[0]
    ms = jnp.mean(x * x, axis=-1, keepdims=True)
    h = (x * lax.rsqrt(ms + NORM_EPS) * ng_ref[...]).astype(_BF16)
    h_ref[0] = h

    base = CONV_HALO - (CONV_WIDTH - 1)

    for p in range(n_pieces):
        val = _dot(h, wv_ref[p])
        glu = _dot(h, wg_ref[p])
        sh[0, 0:CONV_HALO, :] = halo[p]
        sh[0, CONV_HALO:CONV_HALO + TM, :] = val * _sigmoid(glu)
        halo[p] = sh[0, TM:TM + CONV_HALO, :]
        for r in range(1, SUBLANES):
            sh[r, 0:SH_ROWS, :] = sh[0, r:r + SH_ROWS, :]
        for half in range(PW // LANES):
            lane = slice(half * LANES, (half + 1) * LANES)
            out_lane = slice(p * PW + half * LANES, p * PW + (half + 1) * LANES)
            for r0 in range(0, TM, CONV_ROWS):
                acc = jnp.broadcast_to(cb_ref[:, out_lane], (CONV_ROWS, LANES))
                for k in range(CONV_WIDTH):
                    r = (k + base) % SUBLANES
                    q = r0 + k + base - r
                    acc = acc + sh[r, q:q + CONV_ROWS, lane] * cw_ref[k:k + 1, out_lane]
                co_ref[0, r0:r0 + CONV_ROWS, out_lane] = acc


def _conv_front(x, norm_g, w_val, w_glu, conv_w, conv_b):
    bsz, seq, _ = x.shape
    return pl.pallas_call(
        _conv_front_kernel,
        name="conv_front",
        grid=(bsz, seq // TM),
        in_specs=[
            pl.BlockSpec((1, TM, D_MODEL), lambda b, s: (b, s, 0)),
            _resident((1, D_MODEL)),
            _resident((D_CONV // PW, D_MODEL, PW)),
            _resident((D_CONV // PW, D_MODEL, PW)),
            _resident((CONV_WIDTH, D_CONV)),
            _resident((1, D_CONV)),
        ],
        out_specs=[
            pl.BlockSpec((1, TM, D_MODEL), lambda b, s: (b, s, 0)),
            pl.BlockSpec((1, TM, D_CONV), lambda b, s: (b, s, 0)),
        ],
        out_shape=[
            jax.ShapeDtypeStruct((bsz, seq, D_MODEL), _BF16),
            jax.ShapeDtypeStruct((bsz, seq, D_CONV), _F32),
        ],
        scratch_shapes=[pltpu.VMEM((SUBLANES, CONV_HALO + TM, PW), _F32),
                        pltpu.VMEM((D_CONV // PW, CONV_HALO, PW), _F32)],
        compiler_params=pltpu.CompilerParams(
            dimension_semantics=("arbitrary", "arbitrary"), vmem_limit_bytes=VMEM_LIMIT),
    )(x, norm_g, w_val, w_glu, conv_w, conv_b)


def _conv_back_kernel(co_ref, h_ref, lg_ref, lb_ref, wcg_ref, wco_ref, wgc_ref, mc_ref, y_s, a_s):
    co = co_ref[...]
    mu = jnp.mean(co, axis=-1, keepdims=True)
    xc = co - mu
    var = jnp.mean(xc * xc, axis=-1, keepdims=True)
    y = xc * lax.rsqrt(var + LN_EPS) * lg_ref[...] + lb_ref[...]
    y_s[...] = _silu(y)
    h = h_ref[...]
    for j in range(D_CONV // CW):
        cols = slice(j * CW, (j + 1) * CW)
        gate = _dot(h, wcg_ref[j])
        a_s[:, cols] = (y_s[:, cols] * _silu(gate)).astype(_BF16)
    a = a_s[...]
    for j in range(D_MODEL // CW):
        cols = slice(j * CW, (j + 1) * CW)
        yc = _dot(a, wco_ref[j])
        gc = _dot(h, wgc_ref[j])
        mc_ref[:, cols] = _sigmoid(gc) * yc


def _conv_back(co, h, ln_g, ln_b, w_cgate, w_conv_out, w_gconv):
    m = co.shape[0]
    return pl.pallas_call(
        _conv_back_kernel,
        name="conv_back",
        grid=(m // TM,),
        in_specs=[
            pl.BlockSpec((TM, D_CONV), lambda i: (i, 0)),
            pl.BlockSpec((TM, D_MODEL), lambda i: (i, 0)),
            _resident((1, D_CONV)),
            _resident((1, D_CONV)),
            _resident((D_CONV // CW, D_MODEL, CW)),
            _resident((D_MODEL // CW, D_CONV, CW)),
            _resident((D_MODEL // CW, D_MODEL, CW)),
        ],
        out_specs=pl.BlockSpec((TM, D_MODEL), lambda i: (i, 0)),
        out_shape=jax.ShapeDtypeStruct((m, D_MODEL), _F32),
        scratch_shapes=[pltpu.VMEM((TM, D_CONV), _F32), pltpu.VMEM((TM, D_CONV), _BF16)],
        compiler_params=pltpu.CompilerParams(
            dimension_semantics=("arbitrary",), vmem_limit_bytes=VMEM_LIMIT),
    )(co, h, ln_g, ln_b, w_cgate, w_conv_out, w_gconv)


def _rnn_front_kernel(h_ref, wfirst_ref, wpair_ref, wlast_ref, rcw_ref, rcb_ref, wblk_ref, ba_ref, bx_ref,
                      lam_ref, o_ref, rbuf, a_s, u_s, carry_s):
    s = pl.program_id(1)

    @pl.when(s == 0)
    def _():
        rbuf[0:RNN_HALO, :] = jnp.zeros((RNN_HALO, D_RNN), _F32)
        carry_s[...] = jnp.zeros((SUBLANES, D_RNN), _F32)

    h = h_ref[0]
    row_id = lax.broadcasted_iota(jnp.int32, (SUBLANES, RNN_GROUP), 0)
    base = RNN_HALO - (RNN_CONV_WIDTH - 1)

    n_groups = D_RNN // RNN_GROUP
    rx_next = _dot(h, wfirst_ref[...])
    for g in range(n_groups):
        gcols = slice(g * RNN_GROUP, (g + 1) * RNN_GROUP)
        rbuf[RNN_HALO:RNN_HALO + TM, gcols] = rx_next

        v = jnp.broadcast_to(rcb_ref[:, gcols], (TM, RNN_GROUP))
        for k in range(RNN_CONV_WIDTH):
            v = v + rbuf[base + k:base + k + TM, gcols] * rcw_ref[k:k + 1, gcols]
        vb = v.astype(_BF16)

        pieces_a = [None] * 5
        pieces_x = [None] * 5
        for w in range(4):
            p = _dot(vb[:, w * LANES:w * LANES + RG_WIN], wblk_ref[4 * g + w])
            for half in range(2):
                pa = p[:, half * LANES:(half + 1) * LANES]
                px = p[:, RG_WIN + half * LANES:RG_WIN + (half + 1) * LANES]
                idx = w + half
                pieces_a[idx] = pa if pieces_a[idx] is None else pieces_a[idx] + pa
                pieces_x[idx] = px if pieces_x[idx] is None else pieces_x[idx] + px
        pre_a = jnp.concatenate(pieces_a, axis=1) + ba_ref[:, gcols]
        pre_x = jnp.concatenate(pieces_x, axis=1) + bx_ref[:, gcols]

        r_gate = _sigmoid(pre_a)
        i_gate = _sigmoid(pre_x)
        neg_lam = -lam_ref[:, gcols]
        softplus = jnp.maximum(neg_lam, 0.0) + jnp.log1p(jnp.exp(-jnp.abs(neg_lam)))
        log_a = r_gate * (-RG_C * softplus)
        a = jnp.exp(log_a)
        mult = jnp.sqrt(jnp.maximum(jnp.tanh(-log_a) * (1.0 + a * a), 0.0))
        a_s[:, gcols] = a
        u_s[:, gcols] = mult * (i_gate * v)

        hprev = carry_s[:, gcols]
        for r0 in range(0, TM, SUBLANES):
            aa = a_s[r0:r0 + SUBLANES, gcols]
            uu = u_s[r0:r0 + SUBLANES, gcols]
            for d in (1, 2, 4):
                keep = row_id >= d
                u_sh = jnp.where(keep, pltpu.roll(uu, d, axis=0), 0.0)
                a_sh = jnp.where(keep, pltpu.roll(aa, d, axis=0), 1.0)
                uu = uu + aa * u_sh
                aa = aa * a_sh
            hh = uu + aa * hprev
            u_s[r0:r0 + SUBLANES, gcols] = hh
            hprev = jnp.broadcast_to(hh[SUBLANES - 1:SUBLANES, :], (SUBLANES, RNN_GROUP))
        carry_s[:, gcols] = hprev

        if g + 1 < n_groups:
            pair = _dot(h, wpair_ref[g])
            gate, rx_next = pair[:, 0:RNN_GROUP], pair[:, RNN_GROUP:2 * RNN_GROUP]
        else:
            gate = _dot(h, wlast_ref[...])
        o_ref[0, :, gcols] = (u_s[:, gcols] * _silu(gate)).astype(_BF16)

    rbuf[0:RNN_HALO, :] = rbuf[TM:TM + RNN_HALO, :]


def _rnn_front(h, w_first, w_pair, w_last, rnn_conv_w, rnn_conv_b, w_blk, b_a, b_x, lam):
    bsz, seq, _ = h.shape
    return pl.pallas_call(
        _rnn_front_kernel,
        name="rnn_front",
        grid=(bsz, seq // TM),
        in_specs=[
            pl.BlockSpec((1, TM, D_MODEL), lambda b, s: (b, s, 0)),
            _resident((D_MODEL, RNN_GROUP)),
            _resident((D_RNN // RNN_GROUP - 1, D_MODEL, 2 * RNN_GROUP)),
            _resident((D_MODEL, RNN_GROUP)),
            _resident((RNN_CONV_WIDTH, D_RNN)),
            _resident((1, D_RNN)),
            _resident((RNN_BLOCKS, RG_WIN, 2 * RG_WIN)),
            _resident((1, D_RNN)),
            _resident((1, D_RNN)),
            _resident((1, D_RNN)),
        ],
        out_specs=pl.BlockSpec((1, TM, D_RNN), lambda b, s: (b, s, 0)),
        out_shape=jax.ShapeDtypeStruct((bsz, seq, D_RNN), _BF16),
        scratch_shapes=[
            pltpu.VMEM((RNN_HALO + TM, D_RNN), _F32),
            pltpu.VMEM((TM, D_RNN), _F32),
            pltpu.VMEM((TM, D_RNN), _F32),
            pltpu.VMEM((SUBLANES, D_RNN), _F32),
        ],
        compiler_params=pltpu.CompilerParams(
            dimension_semantics=("arbitrary", "arbitrary"), vmem_limit_bytes=VMEM_LIMIT),
    )(h, w_first, w_pair, w_last, rnn_conv_w, rnn_conv_b, w_blk, b_a, b_x, lam)


def _merge_out_kernel(o_ref, h_ref, mc_ref, x_ref, wro_ref, wgr_ref, wout_ref, fg_ref,
                      out_ref, m_s, res_s):
    o = o_ref[...]
    h = h_ref[...]
    for j in range(D_MODEL // CW):
        cols = slice(j * CW, (j + 1) * CW)
        yr = _dot(o, wro_ref[j])
        gr = _dot(h, wgr_ref[j])
        m_s[:, cols] = (mc_ref[:, cols] + _sigmoid(gr) * yr).astype(_BF16)
    merged = m_s[...]
    for j in range(D_MODEL // CW):
        cols = slice(j * CW, (j + 1) * CW)
        res_s[:, cols] = x_ref[:, cols] + _dot(merged, wout_ref[j])
    res = res_s[...]
    ms = jnp.mean(res * res, axis=-1, keepdims=True)
    out_ref[...] = res * lax.rsqrt(ms + NORM_EPS) * fg_ref[...]


def _merge_out(o, h, mc, x, w_rnn_out, w_grnn, w_out, final_g):
    m = x.shape[0]
    return pl.pallas_call(
        _merge_out_kernel,
        name="merge_out",
        grid=(m // TM,),
        in_specs=[
            pl.BlockSpec((TM, D_RNN), lambda i: (i, 0)),
            pl.BlockSpec((TM, D_MODEL), lambda i: (i, 0)),
            pl.BlockSpec((TM, D_MODEL), lambda i: (i, 0)),
            pl.BlockSpec((TM, D_MODEL), lambda i: (i, 0)),
            _resident((D_MODEL // CW, D_RNN, CW)),
            _resident((D_MODEL // CW, D_MODEL, CW)),
            _resident((D_MODEL // CW, D_MODEL, CW)),
            _resident((1, D_MODEL)),
        ],
        out_specs=pl.BlockSpec((TM, D_MODEL), lambda i: (i, 0)),
        out_shape=jax.ShapeDtypeStruct((m, D_MODEL), _F32),
        scratch_shapes=[pltpu.VMEM((TM, D_MODEL), _BF16), pltpu.VMEM((TM, D_MODEL), _F32)],
        compiler_params=pltpu.CompilerParams(
            dimension_semantics=("arbitrary",), vmem_limit_bytes=VMEM_LIMIT),
    )(o, h, mc, x, w_rnn_out, w_grnn, w_out, final_g)


def _col_pieces(w, width):
    k, n = w.shape
    return w.reshape(k, n // width, width).transpose(1, 0, 2).astype(_BF16)


def _window_gate_weights(w_a, w_x):
    outs = []
    for blk in range(RNN_BLOCKS):
        off = (blk * RNN_BLOCK_DIM) % LANES
        pad = ((off, RG_WIN - RNN_BLOCK_DIM - off),) * 2
        outs.append(jnp.concatenate([jnp.pad(w_a[blk], pad), jnp.pad(w_x[blk], pad)], axis=1))
    return jnp.stack(outs).astype(_BF16)


def _layer(x, norm_g, w_in, conv_dw_w, conv_dw_b, conv_ln_g, conv_ln_b, w_conv_out,
           rnn_conv_w, rnn_conv_b, w_rg_a, b_rg_a, w_rg_x, b_rg_x, rg_lambda,
           w_rnn_out, w_out, out_g):
    bsz, seq, _ = x.shape
    m = bsz * seq
    row = lambda v: v.reshape(1, -1).astype(_F32)
    bounds = [0]
    for width in (D_CONV, D_CONV, D_CONV, D_RNN, D_RNN, D_MODEL, D_MODEL):
        bounds.append(bounds[-1] + width)
    seg = [w_in[:, bounds[i]:bounds[i + 1]] for i in range(7)]
    w_val, w_glu, w_cgate, w_rx, w_rgate, w_gconv, w_grnn = seg
    n_groups = D_RNN // RNN_GROUP
    group = lambda w, g: w[:, g * RNN_GROUP:(g + 1) * RNN_GROUP]
    w_pair = jnp.stack([jnp.concatenate([group(w_rgate, g), group(w_rx, g + 1)], axis=1)
                        for g in range(n_groups - 1)]).astype(_BF16)

    h, co = _conv_front(x, row(norm_g), _col_pieces(w_val, PW), _col_pieces(w_glu, PW),
                        conv_dw_w.astype(_F32), row(conv_dw_b))
    mc = _conv_back(co.reshape(m, D_CONV), h.reshape(m, D_MODEL), row(conv_ln_g), row(conv_ln_b),
                    _col_pieces(w_cgate, CW), _col_pieces(w_conv_out, CW), _col_pieces(w_gconv, CW))
    o = _rnn_front(h, group(w_rx, 0).astype(_BF16), w_pair, group(w_rgate, n_groups - 1).astype(_BF16),
                   rnn_conv_w.astype(_F32), row(rnn_conv_b),
                   _window_gate_weights(w_rg_a, w_rg_x), row(b_rg_a), row(b_rg_x), row(rg_lambda))
    out = _merge_out(o.reshape(m, D_RNN), h.reshape(m, D_MODEL), mc, x.reshape(m, D_MODEL),
                     _col_pieces(w_rnn_out, CW), _col_pieces(w_grnn, CW), _col_pieces(w_out, CW),
                     row(out_g))
    return out.reshape(bsz, seq, D_MODEL)


def kernel(x, norm_g, w_in, conv_dw_w, conv_dw_b, conv_ln_g, conv_ln_b, w_conv_out, rnn_conv_w,
           rnn_conv_b, w_rg_a, b_rg_a, w_rg_x, b_rg_x, rg_lambda, w_rnn_out, w_out, final_norm_g):
    depth = norm_g.shape[0]
    assert depth == 1, "the final RMSNorm is fused into the single layer's last call"
    return _layer(x, norm_g[0], w_in[0], conv_dw_w[0], conv_dw_b[0], conv_ln_g[0], conv_ln_b[0],
                  w_conv_out[0], rnn_conv_w[0], rnn_conv_b[0], w_rg_a[0], b_rg_a[0], w_rg_x[0],
                  b_rg_x[0], rg_lambda[0], w_rnn_out[0], w_out[0], final_norm_g)
```

```python
import functools

import jax
import jax.numpy as jnp
from jax import lax
from jax.experimental import pallas as pl
from jax.experimental.pallas import tpu as pltpu

D_MODEL = 2048
D_CONV = 2048
CONV_WIDTH = 31
RNN_BLOCKS = 16
RNN_BLOCK_DIM = 160
D_RNN = RNN_BLOCKS * RNN_BLOCK_DIM
RNN_CONV_WIDTH = 4
RG_C = 8.0
NORM_EPS = 1e-6
LN_EPS = 1e-5

LANES = 128
SUBLANES = 8
VMEM_LIMIT = 56 * 1024 * 1024

TM = 256
CW = 512
PW = 256
CONV_HALO = 32
CONV_ROWS = 32
SH_ROWS = TM + CONV_HALO - SUBLANES
RNN_HALO = 8
RNN_GROUP = 4 * RNN_BLOCK_DIM
RG_WIN = 2 * LANES
SCAN_SEGS = SUBLANES
SCAN_STEPS = TM // SCAN_SEGS
SCAN_PITCH = SCAN_STEPS + SUBLANES
SCAN_SLAB = SCAN_SEGS * SCAN_PITCH
W_PAD = LANES

_F32 = jnp.float32
_BF16 = jnp.bfloat16


def _dot(a, b):
    return jnp.dot(a, b, preferred_element_type=_F32)


def _sigmoid(x):
    return jax.nn.sigmoid(x)


def _silu(x):
    return x * jax.nn.sigmoid(x)


def _resident(shape):
    nd = len(shape)
    return pl.BlockSpec(shape, lambda *_: (0,) * nd, pipeline_mode=pl.Buffered(1))


def _conv_front_kernel(x_ref, ng_ref, wv_ref, wg_ref, cw_ref, cb_ref, h_ref, co_ref, sh, halo):
    s = pl.program_id(1)
    n_pieces = D_CONV // PW

    @pl.when(s == 0)
    def _():
        halo[...] = jnp.zeros(halo.shape, _F32)

    x = x_ref[0]
    ms = jnp.mean(x * x, axis=-1, keepdims=True)
    h = (x * lax.rsqrt(ms + NORM_EPS) * ng_ref[...]).astype(_BF16)
    h_ref[0] = h

    base = CONV_HALO - (CONV_WIDTH - 1)

    for p in range(n_pieces):
        cols = slice(p * PW, (p + 1) * PW)
        val = _dot(h, wv_ref[:, cols])
        glu = _dot(h, wg_ref[:, cols])
        sh[0, 0:CONV_HALO, :] = halo[p]
        sh[0, CONV_HALO:CONV_HALO + TM, :] = val * _sigmoid(glu)
        halo[p] = sh[0, TM:TM + CONV_HALO, :]
        for r in range(1, SUBLANES):
            sh[r, 0:SH_ROWS, :] = sh[0, r:r + SH_ROWS, :]
        for half in range(PW // LANES):
            lane = slice(half * LANES, (half + 1) * LANES)
            out_lane = slice(p * PW + half * LANES, p * PW + (half + 1) * LANES)
            for r0 in range(0, TM, CONV_ROWS):
                acc = jnp.broadcast_to(cb_ref[:, out_lane], (CONV_ROWS, LANES))
                for k in range(CONV_WIDTH):
                    r = (k + base) % SUBLANES
                    q = r0 + k + base - r
                    acc = acc + sh[r, q:q + CONV_ROWS, lane] * cw_ref[k:k + 1, out_lane]
                co_ref[0, r0:r0 + CONV_ROWS, out_lane] = acc


def _conv_front(x, norm_g, w_val, w_glu, conv_w, conv_b):
    bsz, seq, _ = x.shape
    return pl.pallas_call(
        _conv_front_kernel,
        name="conv_front",
        grid=(bsz, seq // TM),
        in_specs=[
            pl.BlockSpec((1, TM, D_MODEL), lambda b, s: (b, s, 0)),
            _resident((1, D_MODEL)),
            _resident((D_MODEL, D_CONV + W_PAD)),
            _resident((D_MODEL, D_CONV + W_PAD)),
            _resident((CONV_WIDTH, D_CONV)),
            _resident((1, D_CONV)),
        ],
        out_specs=[
            pl.BlockSpec((1, TM, D_MODEL), lambda b, s: (b, s, 0)),
            pl.BlockSpec((1, TM, D_CONV), lambda b, s: (b, s, 0)),
        ],
        out_shape=[
            jax.ShapeDtypeStruct((bsz, seq, D_MODEL), _BF16),
            jax.ShapeDtypeStruct((bsz, seq, D_CONV), _F32),
        ],
        scratch_shapes=[pltpu.VMEM((SUBLANES, CONV_HALO + TM, PW), _F32),
                        pltpu.VMEM((D_CONV // PW, CONV_HALO, PW), _F32)],
        compiler_params=pltpu.CompilerParams(
            dimension_semantics=("arbitrary", "arbitrary"), vmem_limit_bytes=VMEM_LIMIT),
    )(x, norm_g, w_val, w_glu, conv_w, conv_b)


def _conv_back_kernel(co_ref, h_ref, lg_ref, lb_ref, wcg_ref, wco_ref, wgc_ref, mc_ref, y_s, a_s):
    co = co_ref[...]
    mu = jnp.mean(co, axis=-1, keepdims=True)
    xc = co - mu
    var = jnp.mean(xc * xc, axis=-1, keepdims=True)
    y = xc * lax.rsqrt(var + LN_EPS) * lg_ref[...] + lb_ref[...]
    y_s[...] = _silu(y)
    h = h_ref[...]
    for j in range(D_CONV // CW):
        cols = slice(j * CW, (j + 1) * CW)
        gate = _dot(h, wcg_ref[:, cols])
        a_s[:, cols] = (y_s[:, cols] * _silu(gate)).astype(_BF16)
    a = a_s[...]
    for j in range(D_MODEL // CW):
        cols = slice(j * CW, (j + 1) * CW)
        yc = _dot(a, wco_ref[:, cols])
        gc = _dot(h, wgc_ref[:, cols])
        mc_ref[:, cols] = _sigmoid(gc) * yc


def _conv_back(co, h, ln_g, ln_b, w_cgate, w_conv_out, w_gconv):
    m = co.shape[0]
    return pl.pallas_call(
        _conv_back_kernel,
        name="conv_back",
        grid=(m // TM,),
        in_specs=[
            pl.BlockSpec((TM, D_CONV), lambda i: (i, 0)),
            pl.BlockSpec((TM, D_MODEL), lambda i: (i, 0)),
            _resident((1, D_CONV)),
            _resident((1, D_CONV)),
            _resident((D_MODEL, D_CONV + W_PAD)),
            _resident((D_CONV, D_MODEL + W_PAD)),
            _resident((D_MODEL, D_MODEL + W_PAD)),
        ],
        out_specs=pl.BlockSpec((TM, D_MODEL), lambda i: (i, 0)),
        out_shape=jax.ShapeDtypeStruct((m, D_MODEL), _F32),
        scratch_shapes=[pltpu.VMEM((TM, D_CONV), _F32), pltpu.VMEM((TM, D_CONV), _BF16)],
        compiler_params=pltpu.CompilerParams(
            dimension_semantics=("arbitrary",), vmem_limit_bytes=VMEM_LIMIT),
    )(co, h, ln_g, ln_b, w_cgate, w_conv_out, w_gconv)


def _segment_scan(a, u, carry_ref, gcols, a_s, u_s, hl_s, p_s, slab0):
    row_id = lax.broadcasted_iota(jnp.int32, (SUBLANES, LANES), 0)
    outs = []
    for c in range(RNN_GROUP // LANES):
        lanes = slice(c * LANES, (c + 1) * LANES)
        clanes = slice(gcols.start + c * LANES, gcols.start + (c + 1) * LANES)
        pbase = (slab0 + c) * SCAN_SLAB
        dbase = (slab0 + c) * TM
        for sg in range(SCAN_SEGS):
            rows = slice(sg * SCAN_STEPS, (sg + 1) * SCAN_STEPS)
            dst = slice(pbase + sg * SCAN_PITCH, pbase + sg * SCAN_PITCH + SCAN_STEPS)
            a_s[dst, :] = a[rows, lanes]
            u_s[dst, :] = u[rows, lanes]
        hloc = jnp.zeros((SUBLANES, LANES), _F32)
        prod = jnp.ones((SUBLANES, LANES), _F32)
        for j in range(SCAN_STEPS):
            aj = a_s[pl.ds(pbase + j, SCAN_SEGS, stride=SCAN_PITCH), :]
            uj = u_s[pl.ds(pbase + j, SCAN_SEGS, stride=SCAN_PITCH), :]
            hloc = aj * hloc + uj
            prod = aj * prod
            hl_s[dbase + j * SUBLANES:dbase + (j + 1) * SUBLANES, :] = hloc
            p_s[dbase + j * SUBLANES:dbase + (j + 1) * SUBLANES, :] = prod
        seg_decay, seg_state = prod, hloc
        for d in (1, 2, 4):
            keep = row_id >= d
            s_sh = jnp.where(keep, pltpu.roll(seg_state, d, axis=0), 0.0)
            d_sh = jnp.where(keep, pltpu.roll(seg_decay, d, axis=0), 1.0)
            seg_state = seg_state + seg_decay * s_sh
            seg_decay = seg_decay * d_sh
        carry_in = carry_ref[:, clanes]
        h_end = seg_state + seg_decay * carry_in
        h_in = jnp.where(row_id == 0, carry_in, pltpu.roll(h_end, 1, axis=0))
        carry_ref[:, clanes] = jnp.broadcast_to(h_end[SUBLANES - 1:SUBLANES, :], (SUBLANES, LANES))
        for j in range(SCAN_STEPS):
            step = slice(dbase + j * SUBLANES, dbase + (j + 1) * SUBLANES)
            u_s[pl.ds(pbase + j, SCAN_SEGS, stride=SCAN_PITCH), :] = hl_s[step, :] + p_s[step, :] * h_in
        outs.append(jnp.concatenate(
            [u_s[pbase + sg * SCAN_PITCH:pbase + sg * SCAN_PITCH + SCAN_STEPS, :]
             for sg in range(SCAN_SEGS)], axis=0))
    return jnp.concatenate(outs, axis=1)


def _rnn_front_kernel(h_ref, wrnn_ref, rcw_ref, rcb_ref, wblk_ref, ba_ref, bx_ref, lam_ref, o_ref,
                      rbuf, a_s, u_s, hl_s, p_s, carry_s):
    s = pl.program_id(1)

    @pl.when(s == 0)
    def _():
        rbuf[0:RNN_HALO, :] = jnp.zeros((RNN_HALO, D_RNN), _F32)
        carry_s[...] = jnp.zeros((SUBLANES, D_RNN), _F32)

    h = h_ref[0]
    base = RNN_HALO - (RNN_CONV_WIDTH - 1)

    n_groups = D_RNN // RNN_GROUP
    rx_next = _dot(h, wrnn_ref[:, 0:RNN_GROUP])
    for g in range(n_groups):
        gcols = slice(g * RNN_GROUP, (g + 1) * RNN_GROUP)
        rbuf[RNN_HALO:RNN_HALO + TM, gcols] = rx_next

        v = jnp.broadcast_to(rcb_ref[:, gcols], (TM, RNN_GROUP))
        for k in range(RNN_CONV_WIDTH):
            v = v + rbuf[base + k:base + k + TM, gcols] * rcw_ref[k:k + 1, gcols]
        vb = v.astype(_BF16)

        pieces_a = [None] * 5
        pieces_x = [None] * 5
        for w in range(4):
            p = _dot(vb[:, w * LANES:w * LANES + RG_WIN], wblk_ref[4 * g + w])
            for half in range(2):
                pa = p[:, half * LANES:(half + 1) * LANES]
                px = p[:, RG_WIN + half * LANES:RG_WIN + (half + 1) * LANES]
                idx = w + half
                pieces_a[idx] = pa if pieces_a[idx] is None else pieces_a[idx] + pa
                pieces_x[idx] = px if pieces_x[idx] is None else pieces_x[idx] + px
        pre_a = jnp.concatenate(pieces_a, axis=1) + ba_ref[:, gcols]
        pre_x = jnp.concatenate(pieces_x, axis=1) + bx_ref[:, gcols]

        r_gate = _sigmoid(pre_a)
        i_gate = _sigmoid(pre_x)
        neg_lam = -lam_ref[:, gcols]
        softplus = jnp.maximum(neg_lam, 0.0) + jnp.log1p(jnp.exp(-jnp.abs(neg_lam)))
        log_a = r_gate * (-RG_C * softplus)
        a = jnp.exp(log_a)
        mult = jnp.sqrt(jnp.maximum(jnp.tanh(-log_a) * (1.0 + a * a), 0.0))
        hseq = _segment_scan(a, mult * (i_gate * v), carry_s, gcols, a_s, u_s, hl_s, p_s,
                             g * (RNN_GROUP // LANES))

        lo = (2 * g + 1) * RNN_GROUP
        if g + 1 < n_groups:
            pair = _dot(h, wrnn_ref[:, lo:lo + 2 * RNN_GROUP])
            gate, rx_next = pair[:, 0:RNN_GROUP], pair[:, RNN_GROUP:2 * RNN_GROUP]
        else:
            gate = _dot(h, wrnn_ref[:, lo:lo + RNN_GROUP])
        o_ref[0, :, gcols] = (hseq * _silu(gate)).astype(_BF16)

    rbuf[0:RNN_HALO, :] = rbuf[TM:TM + RNN_HALO, :]


def _rnn_front(h, w_rnn, rnn_conv_w, rnn_conv_b, w_blk, b_a, b_x, lam):
    bsz, seq, _ = h.shape
    return pl.pallas_call(
        _rnn_front_kernel,
        name="rnn_front",
        grid=(bsz, seq // TM),
        in_specs=[
            pl.BlockSpec((1, TM, D_MODEL), lambda b, s: (b, s, 0)),
            _resident((D_MODEL, 2 * D_RNN + W_PAD)),
            _resident((RNN_CONV_WIDTH, D_RNN)),
            _resident((1, D_RNN)),
            _resident((RNN_BLOCKS, RG_WIN, 2 * RG_WIN)),
            _resident((1, D_RNN)),
            _resident((1, D_RNN)),
            _resident((1, D_RNN)),
        ],
        out_specs=pl.BlockSpec((1, TM, D_RNN), lambda b, s: (b, s, 0)),
        out_shape=jax.ShapeDtypeStruct((bsz, seq, D_RNN), _BF16),
        scratch_shapes=[
            pltpu.VMEM((RNN_HALO + TM, D_RNN), _F32),
            pltpu.VMEM((D_RNN // LANES * SCAN_SLAB, LANES), _F32),
            pltpu.VMEM((D_RNN // LANES * SCAN_SLAB, LANES), _F32),
            pltpu.VMEM((D_RNN // LANES * TM, LANES), _F32),
            pltpu.VMEM((D_RNN // LANES * TM, LANES), _F32),
            pltpu.VMEM((SUBLANES, D_RNN), _F32),
        ],
        compiler_params=pltpu.CompilerParams(
            dimension_semantics=("arbitrary", "arbitrary"), vmem_limit_bytes=VMEM_LIMIT),
    )(h, w_rnn, rnn_conv_w, rnn_conv_b, w_blk, b_a, b_x, lam)


def _merge_out_kernel(o_ref, h_ref, mc_ref, x_ref, wro_ref, wgr_ref, wout_ref, fg_ref,
                      out_ref, m_s, res_s):
    o = o_ref[...]
    h = h_ref[...]
    for j in range(D_MODEL // CW):
        cols = slice(j * CW, (j + 1) * CW)
        yr = _dot(o, wro_ref[:, cols])
        gr = _dot(h, wgr_ref[:, cols])
        m_s[:, cols] = (mc_ref[:, cols] + _sigmoid(gr) * yr).astype(_BF16)
    merged = m_s[...]
    for j in range(D_MODEL // CW):
        cols = slice(j * CW, (j + 1) * CW)
        res_s[:, cols] = x_ref[:, cols] + _dot(merged, wout_ref[:, cols])
    res = res_s[...]
    ms = jnp.mean(res * res, axis=-1, keepdims=True)
    out_ref[...] = res * lax.rsqrt(ms + NORM_EPS) * fg_ref[...]


def _merge_out(o, h, mc, x, w_rnn_out, w_grnn, w_out, final_g):
    m = x.shape[0]
    return pl.pallas_call(
        _merge_out_kernel,
        name="merge_out",
        grid=(m // TM,),
        in_specs=[
            pl.BlockSpec((TM, D_RNN), lambda i: (i, 0)),
            pl.BlockSpec((TM, D_MODEL), lambda i: (i, 0)),
            pl.BlockSpec((TM, D_MODEL), lambda i: (i, 0)),
            pl.BlockSpec((TM, D_MODEL), lambda i: (i, 0)),
            _resident((D_RNN, D_MODEL + W_PAD)),
            _resident((D_MODEL, D_MODEL + W_PAD)),
            _resident((D_MODEL, D_MODEL + W_PAD)),
            _resident((1, D_MODEL)),
        ],
        out_specs=pl.BlockSpec((TM, D_MODEL), lambda i: (i, 0)),
        out_shape=jax.ShapeDtypeStruct((m, D_MODEL), _F32),
        scratch_shapes=[pltpu.VMEM((TM, D_MODEL), _BF16), pltpu.VMEM((TM, D_MODEL), _F32)],
        compiler_params=pltpu.CompilerParams(
            dimension_semantics=("arbitrary",), vmem_limit_bytes=VMEM_LIMIT),
    )(o, h, mc, x, w_rnn_out, w_grnn, w_out, final_g)


def _padded(w):
    return jnp.pad(w.astype(_BF16), ((0, 0), (0, W_PAD)))


def _window_gate_weights(w_a, w_x):
    outs = []
    for blk in range(RNN_BLOCKS):
        off = (blk * RNN_BLOCK_DIM) % LANES
        pad = ((off, RG_WIN - RNN_BLOCK_DIM - off),) * 2
        outs.append(jnp.concatenate([jnp.pad(w_a[blk], pad), jnp.pad(w_x[blk], pad)], axis=1))
    return jnp.stack(outs).astype(_BF16)


def _layer(x, norm_g, w_in, conv_dw_w, conv_dw_b, conv_ln_g, conv_ln_b, w_conv_out,
           rnn_conv_w, rnn_conv_b, w_rg_a, b_rg_a, w_rg_x, b_rg_x, rg_lambda,
           w_rnn_out, w_out, out_g):
    bsz, seq, _ = x.shape
    m = bsz * seq
    row = lambda v: v.reshape(1, -1).astype(_F32)
    bounds = [0]
    for width in (D_CONV, D_CONV, D_CONV, D_RNN, D_RNN, D_MODEL, D_MODEL):
        bounds.append(bounds[-1] + width)
    seg = [w_in[:, bounds[i]:bounds[i + 1]] for i in range(7)]
    w_val, w_glu, w_cgate, w_rx, w_rgate, w_gconv, w_grnn = seg
    group = lambda w, g: w[:, g * RNN_GROUP:(g + 1) * RNN_GROUP]
    w_rnn = jnp.concatenate([group(w, g) for g in range(D_RNN // RNN_GROUP) for w in (w_rx, w_rgate)],
                            axis=1)

    h, co = _conv_front(x, row(norm_g), _padded(w_val), _padded(w_glu),
                        conv_dw_w.astype(_F32), row(conv_dw_b))
    mc = _conv_back(co.reshape(m, D_CONV), h.reshape(m, D_MODEL), row(conv_ln_g), row(conv_ln_b),
                    _padded(w_cgate), _padded(w_conv_out), _padded(w_gconv))
    o = _rnn_front(h, _padded(w_rnn), rnn_conv_w.astype(_F32), row(rnn_conv_b),
                   _window_gate_weights(w_rg_a, w_rg_x), row(b_rg_a), row(b_rg_x), row(rg_lambda))
    out = _merge_out(o.reshape(m, D_RNN), h.reshape(m, D_MODEL), mc, x.reshape(m, D_MODEL),
                     _padded(w_rnn_out), _padded(w_grnn), _padded(w_out), row(out_g))
    return out.reshape(bsz, seq, D_MODEL)


def kernel(x, norm_g, w_in, conv_dw_w, conv_dw_b, conv_ln_g, conv_ln_b, w_conv_out, rnn_conv_w,
           rnn_conv_b, w_rg_a, b_rg_a, w_rg_x, b_rg_x, rg_lambda, w_rnn_out, w_out, final_norm_g):
    depth = norm_g.shape[0]
    assert depth == 1, "the final RMSNorm is fused into the single layer's last call"
    return _layer(x, norm_g[0], w_in[0], conv_dw_w[0], conv_dw_b[0], conv_ln_g[0], conv_ln_b[0],
                  w_conv_out[0], rnn_conv_w[0], rnn_conv_b[0], w_rg_a[0], b_rg_a[0], w_rg_x[0],
                  b_rg_x[0], rg_lambda[0], w_rnn_out[0], w_out[0], final_norm_g)
```

```python
import functools

import jax
import jax.numpy as jnp
from jax import lax
from jax.experimental import pallas as pl
from jax.experimental.pallas import tpu as pltpu

D_MODEL = 2048
D_CONV = 2048
CONV_WIDTH = 31
RNN_BLOCKS = 16
RNN_BLOCK_DIM = 160
D_RNN = RNN_BLOCKS * RNN_BLOCK_DIM
RNN_CONV_WIDTH = 4
RG_C = 8.0
NORM_EPS = 1e-6
LN_EPS = 1e-5

LANES = 128
SUBLANES = 8
VMEM_LIMIT = 56 * 1024 * 1024

TM = 256
TMC = 512
CW = 512
PW = 256
CONV_HALO = 32
CONV_ROWS = 32
SH_ROWS = TMC + CONV_HALO - SUBLANES
RNN_HALO = 8
RNN_GROUP = 4 * RNN_BLOCK_DIM
RG_WIN = 2 * LANES
SCAN_SEGS = SUBLANES
SCAN_STEPS = TM // SCAN_SEGS
SCAN_PITCH = SCAN_STEPS + SUBLANES
SCAN_SLAB = SCAN_SEGS * SCAN_PITCH
W_PAD = LANES

_F32 = jnp.float32
_BF16 = jnp.bfloat16


def _dot(a, b):
    return jnp.dot(a, b, preferred_element_type=_F32)


def _sigmoid(x):
    return jax.nn.sigmoid(x)


def _silu(x):
    return x * jax.nn.sigmoid(x)


def _resident(shape):
    nd = len(shape)
    return pl.BlockSpec(shape, lambda *_: (0,) * nd, pipeline_mode=pl.Buffered(1))


def _conv_front_kernel(x_ref, ng_ref, wv_ref, wg_ref, cw_ref, cb_ref, h_ref, co_ref, sh, halo):
    s = pl.program_id(1)
    n_pieces = D_CONV // PW

    @pl.when(s == 0)
    def _():
        halo[...] = jnp.zeros(halo.shape, _F32)

    x = x_ref[0]
    ms = jnp.mean(x * x, axis=-1, keepdims=True)
    h = (x * lax.rsqrt(ms + NORM_EPS) * ng_ref[...]).astype(_BF16)
    h_ref[0] = h

    base = CONV_HALO - (CONV_WIDTH - 1)

    for p in range(n_pieces):
        cols = slice(p * PW, (p + 1) * PW)
        val = _dot(h, wv_ref[:, cols])
        glu = _dot(h, wg_ref[:, cols])
        sh[0, 0:CONV_HALO, :] = halo[p]
        sh[0, CONV_HALO:CONV_HALO + TMC, :] = val * _sigmoid(glu)
        halo[p] = sh[0, TMC:TMC + CONV_HALO, :]
        for r in range(1, SUBLANES):
            sh[r, 0:SH_ROWS, :] = sh[0, r:r + SH_ROWS, :]
        for half in range(PW // LANES):
            lane = slice(half * LANES, (half + 1) * LANES)
            out_lane = slice(p * PW + half * LANES, p * PW + (half + 1) * LANES)
            for r0 in range(0, TMC, CONV_ROWS):
                acc = jnp.broadcast_to(cb_ref[:, out_lane], (CONV_ROWS, LANES))
                for k in range(CONV_WIDTH):
                    r = (k + base) % SUBLANES
                    q = r0 + k + base - r
                    acc = acc + sh[r, q:q + CONV_ROWS, lane] * cw_ref[k:k + 1, out_lane]
                co_ref[0, r0:r0 + CONV_ROWS, out_lane] = acc


def _conv_front(x, norm_g, w_val, w_glu, conv_w, conv_b):
    bsz, seq, _ = x.shape
    return pl.pallas_call(
        _conv_front_kernel,
        name="conv_front",
        grid=(bsz, seq // TMC),
        in_specs=[
            pl.BlockSpec((1, TMC, D_MODEL), lambda b, s: (b, s, 0)),
            _resident((1, D_MODEL)),
            _resident((D_MODEL, D_CONV + W_PAD)),
            _resident((D_MODEL, D_CONV + W_PAD)),
            _resident((CONV_WIDTH, D_CONV)),
            _resident((1, D_CONV)),
        ],
        out_specs=[
            pl.BlockSpec((1, TMC, D_MODEL), lambda b, s: (b, s, 0)),
            pl.BlockSpec((1, TMC, D_CONV), lambda b, s: (b, s, 0)),
        ],
        out_shape=[
            jax.ShapeDtypeStruct((bsz, seq, D_MODEL), _BF16),
            jax.ShapeDtypeStruct((bsz, seq, D_CONV), _F32),
        ],
        scratch_shapes=[pltpu.VMEM((SUBLANES, CONV_HALO + TMC, PW), _F32),
                        pltpu.VMEM((D_CONV // PW, CONV_HALO, PW), _F32)],
        compiler_params=pltpu.CompilerParams(
            dimension_semantics=("arbitrary", "arbitrary"), vmem_limit_bytes=VMEM_LIMIT),
    )(x, norm_g, w_val, w_glu, conv_w, conv_b)


def _conv_back_kernel(co_ref, h_ref, lg_ref, lb_ref, wcg_ref, wco_ref, wgc_ref, mc_ref, y_s, a_s):
    co = co_ref[...]
    mu = jnp.mean(co, axis=-1, keepdims=True)
    xc = co - mu
    var = jnp.mean(xc * xc, axis=-1, keepdims=True)
    y = xc * lax.rsqrt(var + LN_EPS) * lg_ref[...] + lb_ref[...]
    y_s[...] = _silu(y)
    h = h_ref[...]
    for j in range(D_CONV // CW):
        cols = slice(j * CW, (j + 1) * CW)
        gate = _dot(h, wcg_ref[:, cols])
        a_s[:, cols] = (y_s[:, cols] * _silu(gate)).astype(_BF16)
    a = a_s[...]
    for j in range(D_MODEL // CW):
        cols = slice(j * CW, (j + 1) * CW)
        yc = _dot(a, wco_ref[:, cols])
        gc = _dot(h, wgc_ref[:, cols])
        mc_ref[:, cols] = _sigmoid(gc) * yc


def _conv_back(co, h, ln_g, ln_b, w_cgate, w_conv_out, w_gconv):
    m = co.shape[0]
    return pl.pallas_call(
        _conv_back_kernel,
        name="conv_back",
        grid=(m // TM,),
        in_specs=[
            pl.BlockSpec((TM, D_CONV), lambda i: (i, 0)),
            pl.BlockSpec((TM, D_MODEL), lambda i: (i, 0)),
            _resident((1, D_CONV)),
            _resident((1, D_CONV)),
            _resident((D_MODEL, D_CONV + W_PAD)),
            _resident((D_CONV, D_MODEL + W_PAD)),
            _resident((D_MODEL, D_MODEL + W_PAD)),
        ],
        out_specs=pl.BlockSpec((TM, D_MODEL), lambda i: (i, 0)),
        out_shape=jax.ShapeDtypeStruct((m, D_MODEL), _F32),
        scratch_shapes=[pltpu.VMEM((TM, D_CONV), _F32), pltpu.VMEM((TM, D_CONV), _BF16)],
        compiler_params=pltpu.CompilerParams(
            dimension_semantics=("arbitrary",), vmem_limit_bytes=VMEM_LIMIT),
    )(co, h, ln_g, ln_b, w_cgate, w_conv_out, w_gconv)


def _segment_scan(a, u, carry_ref, gcols, a_s, u_s, hl_s, p_s, slab0):
    row_id = lax.broadcasted_iota(jnp.int32, (SUBLANES, LANES), 0)
    outs = []
    for c in range(RNN_GROUP // LANES):
        lanes = slice(c * LANES, (c + 1) * LANES)
        clanes = slice(gcols.start + c * LANES, gcols.start + (c + 1) * LANES)
        pbase = (slab0 + c) * SCAN_SLAB
        dbase = (slab0 + c) * TM
        for sg in range(SCAN_SEGS):
            rows = slice(sg * SCAN_STEPS, (sg + 1) * SCAN_STEPS)
            dst = slice(pbase + sg * SCAN_PITCH, pbase + sg * SCAN_PITCH + SCAN_STEPS)
            a_s[dst, :] = a[rows, lanes]
            u_s[dst, :] = u[rows, lanes]
        hloc = jnp.zeros((SUBLANES, LANES), _F32)
        prod = jnp.ones((SUBLANES, LANES), _F32)
        for j in range(SCAN_STEPS):
            aj = a_s[pl.ds(pbase + j, SCAN_SEGS, stride=SCAN_PITCH), :]
            uj = u_s[pl.ds(pbase + j, SCAN_SEGS, stride=SCAN_PITCH), :]
            hloc = aj * hloc + uj
            prod = aj * prod
            hl_s[dbase + j * SUBLANES:dbase + (j + 1) * SUBLANES, :] = hloc
            p_s[dbase + j * SUBLANES:dbase + (j + 1) * SUBLANES, :] = prod
        seg_decay, seg_state = prod, hloc
        for d in (1, 2, 4):
            keep = row_id >= d
            s_sh = jnp.where(keep, pltpu.roll(seg_state, d, axis=0), 0.0)
            d_sh = jnp.where(keep, pltpu.roll(seg_decay, d, axis=0), 1.0)
            seg_state = seg_state + seg_decay * s_sh
            seg_decay = seg_decay * d_sh
        carry_in = carry_ref[:, clanes]
        h_end = seg_state + seg_decay * carry_in
        h_in = jnp.where(row_id == 0, carry_in, pltpu.roll(h_end, 1, axis=0))
        carry_ref[:, clanes] = jnp.broadcast_to(h_end[SUBLANES - 1:SUBLANES, :], (SUBLANES, LANES))
        for j in range(SCAN_STEPS):
            step = slice(dbase + j * SUBLANES, dbase + (j + 1) * SUBLANES)
            u_s[pl.ds(pbase + j, SCAN_SEGS, stride=SCAN_PITCH), :] = hl_s[step, :] + p_s[step, :] * h_in
        outs.append(jnp.concatenate(
            [u_s[pbase + sg * SCAN_PITCH:pbase + sg * SCAN_PITCH + SCAN_STEPS, :]
             for sg in range(SCAN_SEGS)], axis=0))
    return jnp.concatenate(outs, axis=1)


def _rnn_front_kernel(h_ref, wrnn_ref, rcw_ref, rcb_ref, wblk_ref, ba_ref, bx_ref, lam_ref, o_ref,
                      rbuf, a_s, u_s, hl_s, p_s, carry_s):
    s = pl.program_id(1)

    @pl.when(s == 0)
    def _():
        rbuf[0:RNN_HALO, :] = jnp.zeros((RNN_HALO, D_RNN), _F32)
        carry_s[...] = jnp.zeros((SUBLANES, D_RNN), _F32)

    h = h_ref[0]
    base = RNN_HALO - (RNN_CONV_WIDTH - 1)

    n_groups = D_RNN // RNN_GROUP
    rx_next = _dot(h, wrnn_ref[:, 0:RNN_GROUP])
    for g in range(n_groups):
        gcols = slice(g * RNN_GROUP, (g + 1) * RNN_GROUP)
        rbuf[RNN_HALO:RNN_HALO + TM, gcols] = rx_next

        v = jnp.broadcast_to(rcb_ref[:, gcols], (TM, RNN_GROUP))
        for k in range(RNN_CONV_WIDTH):
            v = v + rbuf[base + k:base + k + TM, gcols] * rcw_ref[k:k + 1, gcols]
        vb = v.astype(_BF16)

        pieces_a = [None] * 5
        pieces_x = [None] * 5
        for w in range(4):
            p = _dot(vb[:, w * LANES:w * LANES + RG_WIN], wblk_ref[4 * g + w])
            for half in range(2):
                pa = p[:, half * LANES:(half + 1) * LANES]
                px = p[:, RG_WIN + half * LANES:RG_WIN + (half + 1) * LANES]
                idx = w + half
                pieces_a[idx] = pa if pieces_a[idx] is None else pieces_a[idx] + pa
                pieces_x[idx] = px if pieces_x[idx] is None else pieces_x[idx] + px
        pre_a = jnp.concatenate(pieces_a, axis=1) + ba_ref[:, gcols]
        pre_x = jnp.concatenate(pieces_x, axis=1) + bx_ref[:, gcols]

        r_gate = _sigmoid(pre_a)
        i_gate = _sigmoid(pre_x)
        neg_lam = -lam_ref[:, gcols]
        softplus = jnp.maximum(neg_lam, 0.0) + jnp.log1p(jnp.exp(-jnp.abs(neg_lam)))
        log_a = r_gate * (-RG_C * softplus)
        a = jnp.exp(log_a)
        mult = jnp.sqrt(jnp.maximum(jnp.tanh(-log_a) * (1.0 + a * a), 0.0))
        hseq = _segment_scan(a, mult * (i_gate * v), carry_s, gcols, a_s, u_s, hl_s, p_s,
                             g * (RNN_GROUP // LANES))

        lo = (2 * g + 1) * RNN_GROUP
        if g + 1 < n_groups:
            pair = _dot(h, wrnn_ref[:, lo:lo + 2 * RNN_GROUP])
            gate, rx_next = pair[:, 0:RNN_GROUP], pair[:, RNN_GROUP:2 * RNN_GROUP]
        else:
            gate = _dot(h, wrnn_ref[:, lo:lo + RNN_GROUP])
        o_ref[0, :, gcols] = (hseq * _silu(gate)).astype(_BF16)

    rbuf[0:RNN_HALO, :] = rbuf[TM:TM + RNN_HALO, :]


def _rnn_front(h, w_rnn, rnn_conv_w, rnn_conv_b, w_blk, b_a, b_x, lam):
    bsz, seq, _ = h.shape
    return pl.pallas_call(
        _rnn_front_kernel,
        name="rnn_front",
        grid=(bsz, seq // TM),
        in_specs=[
            pl.BlockSpec((1, TM, D_MODEL), lambda b, s: (b, s, 0)),
            _resident((D_MODEL, 2 * D_RNN + W_PAD)),
            _resident((RNN_CONV_WIDTH, D_RNN)),
            _resident((1, D_RNN)),
            _resident((RNN_BLOCKS, RG_WIN, 2 * RG_WIN)),
            _resident((1, D_RNN)),
            _resident((1, D_RNN)),
            _resident((1, D_RNN)),
        ],
        out_specs=pl.BlockSpec((1, TM, D_RNN), lambda b, s: (b, s, 0)),
        out_shape=jax.ShapeDtypeStruct((bsz, seq, D_RNN), _BF16),
        scratch_shapes=[
            pltpu.VMEM((RNN_HALO + TM, D_RNN), _F32),
            pltpu.VMEM((D_RNN // LANES * SCAN_SLAB, LANES), _F32),
            pltpu.VMEM((D_RNN // LANES * SCAN_SLAB, LANES), _F32),
            pltpu.VMEM((D_RNN // LANES * TM, LANES), _F32),
            pltpu.VMEM((D_RNN // LANES * TM, LANES), _F32),
            pltpu.VMEM((SUBLANES, D_RNN), _F32),
        ],
        compiler_params=pltpu.CompilerParams(
            dimension_semantics=("arbitrary", "arbitrary"), vmem_limit_bytes=VMEM_LIMIT),
    )(h, w_rnn, rnn_conv_w, rnn_conv_b, w_blk, b_a, b_x, lam)


def _merge_out_kernel(o_ref, h_ref, mc_ref, x_ref, wro_ref, wgr_ref, wout_ref, fg_ref,
                      out_ref, m_s, res_s):
    o = o_ref[...]
    h = h_ref[...]
    for j in range(D_MODEL // CW):
        cols = slice(j * CW, (j + 1) * CW)
        yr = _dot(o, wro_ref[:, cols])
        gr = _dot(h, wgr_ref[:, cols])
        m_s[:, cols] = (mc_ref[:, cols] + _sigmoid(gr) * yr).astype(_BF16)
    merged = m_s[...]
    for j in range(D_MODEL // CW):
        cols = slice(j * CW, (j + 1) * CW)
        res_s[:, cols] = x_ref[:, cols] + _dot(merged, wout_ref[:, cols])
    res = res_s[...]
    ms = jnp.mean(res * res, axis=-1, keepdims=True)
    out_ref[...] = res * lax.rsqrt(ms + NORM_EPS) * fg_ref[...]


def _merge_out(o, h, mc, x, w_rnn_out, w_grnn, w_out, final_g):
    m = x.shape[0]
    return pl.pallas_call(
        _merge_out_kernel,
        name="merge_out",
        grid=(m // TM,),
        in_specs=[
            pl.BlockSpec((TM, D_RNN), lambda i: (i, 0)),
            pl.BlockSpec((TM, D_MODEL), lambda i: (i, 0)),
            pl.BlockSpec((TM, D_MODEL), lambda i: (i, 0)),
            pl.BlockSpec((TM, D_MODEL), lambda i: (i, 0)),
            _resident((D_RNN, D_MODEL + W_PAD)),
            _resident((D_MODEL, D_MODEL + W_PAD)),
            _resident((D_MODEL, D_MODEL + W_PAD)),
            _resident((1, D_MODEL)),
        ],
        out_specs=pl.BlockSpec((TM, D_MODEL), lambda i: (i, 0)),
        out_shape=jax.ShapeDtypeStruct((m, D_MODEL), _F32),
        scratch_shapes=[pltpu.VMEM((TM, D_MODEL), _BF16), pltpu.VMEM((TM, D_MODEL), _F32)],
        compiler_params=pltpu.CompilerParams(
            dimension_semantics=("arbitrary",), vmem_limit_bytes=VMEM_LIMIT),
    )(o, h, mc, x, w_rnn_out, w_grnn, w_out, final_g)


def _padded(w):
    return jnp.pad(w, ((0, 0), (0, W_PAD))).astype(_BF16)


def _window_gate_weights(w_a, w_x):
    outs = []
    for blk in range(RNN_BLOCKS):
        off = (blk * RNN_BLOCK_DIM) % LANES
        pad = ((off, RG_WIN - RNN_BLOCK_DIM - off),) * 2
        outs.append(jnp.concatenate([jnp.pad(w_a[blk], pad), jnp.pad(w_x[blk], pad)], axis=1))
    return jnp.stack(outs).astype(_BF16)


def _layer(x, norm_g, w_in, conv_dw_w, conv_dw_b, conv_ln_g, conv_ln_b, w_conv_out,
           rnn_conv_w, rnn_conv_b, w_rg_a, b_rg_a, w_rg_x, b_rg_x, rg_lambda,
           w_rnn_out, w_out, out_g):
    bsz, seq, _ = x.shape
    m = bsz * seq
    row = lambda v: v.reshape(1, -1).astype(_F32)
    bounds = [0]
    for width in (D_CONV, D_CONV, D_CONV, D_RNN, D_RNN, D_MODEL, D_MODEL):
        bounds.append(bounds[-1] + width)
    seg = [w_in[:, bounds[i]:bounds[i + 1]] for i in range(7)]
    w_val, w_glu, w_cgate, w_rx, w_rgate, w_gconv, w_grnn = seg
    group = lambda w, g: w[:, g * RNN_GROUP:(g + 1) * RNN_GROUP]
    w_rnn = jnp.concatenate([group(w, g) for g in range(D_RNN // RNN_GROUP) for w in (w_rx, w_rgate)],
                            axis=1)

    h, co = _conv_front(x, row(norm_g), _padded(w_val), _padded(w_glu),
                        conv_dw_w.astype(_F32), row(conv_dw_b))
    mc = _conv_back(co.reshape(m, D_CONV), h.reshape(m, D_MODEL), row(conv_ln_g), row(conv_ln_b),
                    _padded(w_cgate), _padded(w_conv_out), _padded(w_gconv))
    o = _rnn_front(h, _padded(w_rnn), rnn_conv_w.astype(_F32), row(rnn_conv_b),
                   _window_gate_weights(w_rg_a, w_rg_x), row(b_rg_a), row(b_rg_x), row(rg_lambda))
    out = _merge_out(o.reshape(m, D_RNN), h.reshape(m, D_MODEL), mc, x.reshape(m, D_MODEL),
                     _padded(w_rnn_out), _padded(w_grnn), _padded(w_out), row(out_g))
    return out.reshape(bsz, seq, D_MODEL)


def kernel(x, norm_g, w_in, conv_dw_w, conv_dw_b, conv_ln_g, conv_ln_b, w_conv_out, rnn_conv_w,
           rnn_conv_b, w_rg_a, b_rg_a, w_rg_x, b_rg_x, rg_lambda, w_rnn_out, w_out, final_norm_g):
    depth = norm_g.shape[0]
    assert depth == 1, "the final RMSNorm is fused into the single layer's last call"
    return _layer(x, norm_g[0], w_in[0], conv_dw_w[0], conv_dw_b[0], conv_ln_g[0], conv_ln_b[0],
                  w_conv_out[0], rnn_conv_w[0], rnn_conv_b[0], w_rg_a[0], b_rg_a[0], w_rg_x[0],
                  b_rg_x[0], rg_lambda[0], w_rnn_out[0], w_out[0], final_norm_g)
```

```python
import functools

import jax
import jax.numpy as jnp
from jax import lax
from jax.experimental import pallas as pl
from jax.experimental.pallas import tpu as pltpu

D_MODEL = 2048
D_CONV = 2048
CONV_WIDTH = 31
RNN_BLOCKS = 16
RNN_BLOCK_DIM = 160
D_RNN = RNN_BLOCKS * RNN_BLOCK_DIM
RNN_CONV_WIDTH = 4
RG_C = 8.0
NORM_EPS = 1e-6
LN_EPS = 1e-5

LANES = 128
SUBLANES = 8
VMEM_LIMIT = 56 * 1024 * 1024

TM = 256
TMC = 512
CW = 512
PW = 256
CONV_HALO = 32
CONV_ROWS = 32
SH_ROWS = TMC + CONV_HALO - SUBLANES
RNN_HALO = 8
RNN_GROUP = 4 * RNN_BLOCK_DIM
RG_WIN = 2 * LANES
SCAN_SEGS = SUBLANES
SCAN_STEPS = TM // SCAN_SEGS
SCAN_PITCH = SCAN_STEPS + SUBLANES
SCAN_SLAB = SCAN_SEGS * SCAN_PITCH
W_PAD = LANES

_F32 = jnp.float32
_BF16 = jnp.bfloat16


def _dot(a, b):
    return jnp.dot(a, b, preferred_element_type=_F32)


def _sigmoid(x):
    return jax.nn.sigmoid(x)


def _silu(x):
    return x * jax.nn.sigmoid(x)


def _resident(shape):
    nd = len(shape)
    return pl.BlockSpec(shape, lambda *_: (0,) * nd, pipeline_mode=pl.Buffered(1))


def _conv_front_kernel(x_ref, ng_ref, wv_ref, wg_ref, cw_ref, cb_ref, h_ref, co_ref, sh, halo):
    s = pl.program_id(1)
    n_pieces = D_CONV // PW

    @pl.when(s == 0)
    def _():
        halo[...] = jnp.zeros(halo.shape, _F32)

    x = x_ref[0]
    ms = jnp.mean(x * x, axis=-1, keepdims=True)
    h = (x * lax.rsqrt(ms + NORM_EPS) * ng_ref[...]).astype(_BF16)
    h_ref[0] = h

    base = CONV_HALO - (CONV_WIDTH - 1)

    for p in range(n_pieces):
        cols = slice(p * PW, (p + 1) * PW)
        val = _dot(h, wv_ref[:, cols])
        glu = _dot(h, wg_ref[:, cols])
        sh[0, 0:CONV_HALO, :] = halo[p]
        sh[0, CONV_HALO:CONV_HALO + TMC, :] = val * _sigmoid(glu)
        halo[p] = sh[0, TMC:TMC + CONV_HALO, :]
        for r in range(1, SUBLANES):
            sh[r, 0:SH_ROWS, :] = sh[0, r:r + SH_ROWS, :]
        for half in range(PW // LANES):
            lane = slice(half * LANES, (half + 1) * LANES)
            out_lane = slice(p * PW + half * LANES, p * PW + (half + 1) * LANES)
            for r0 in range(0, TMC, CONV_ROWS):
                acc = jnp.broadcast_to(cb_ref[:, out_lane], (CONV_ROWS, LANES))
                for k in range(CONV_WIDTH):
                    r = (k + base) % SUBLANES
                    q = r0 + k + base - r
                    acc = acc + sh[r, q:q + CONV_ROWS, lane] * cw_ref[k:k + 1, out_lane]
                co_ref[0, r0:r0 + CONV_ROWS, out_lane] = acc


def _conv_front(x, norm_g, w_val, w_glu, conv_w, conv_b):
    bsz, seq, _ = x.shape
    return pl.pallas_call(
        _conv_front_kernel,
        name="conv_front",
        grid=(bsz, seq // TMC),
        in_specs=[
            pl.BlockSpec((1, TMC, D_MODEL), lambda b, s: (b, s, 0)),
            _resident((1, D_MODEL)),
            _resident((D_MODEL, D_CONV + W_PAD)),
            _resident((D_MODEL, D_CONV + W_PAD)),
            _resident((CONV_WIDTH, D_CONV)),
            _resident((1, D_CONV)),
        ],
        out_specs=[
            pl.BlockSpec((1, TMC, D_MODEL), lambda b, s: (b, s, 0)),
            pl.BlockSpec((1, TMC, D_CONV), lambda b, s: (b, s, 0)),
        ],
        out_shape=[
            jax.ShapeDtypeStruct((bsz, seq, D_MODEL), _BF16),
            jax.ShapeDtypeStruct((bsz, seq, D_CONV), _F32),
        ],
        scratch_shapes=[pltpu.VMEM((SUBLANES, CONV_HALO + TMC, PW), _F32),
                        pltpu.VMEM((D_CONV // PW, CONV_HALO, PW), _F32)],
        compiler_params=pltpu.CompilerParams(
            dimension_semantics=("arbitrary", "arbitrary"), vmem_limit_bytes=VMEM_LIMIT),
    )(x, norm_g, w_val, w_glu, conv_w, conv_b)


def _conv_back_kernel(co_ref, h_ref, lg_ref, lb_ref, wcg_ref, wco_ref, wgc_ref, mc_ref, y_s, a_s):
    co = co_ref[...]
    mu = jnp.mean(co, axis=-1, keepdims=True)
    xc = co - mu
    var = jnp.mean(xc * xc, axis=-1, keepdims=True)
    y = xc * lax.rsqrt(var + LN_EPS) * lg_ref[...] + lb_ref[...]
    y_s[...] = _silu(y)
    h = h_ref[...]
    for j in range(D_CONV // CW):
        cols = slice(j * CW, (j + 1) * CW)
        gate = _dot(h, wcg_ref[:, cols])
        a_s[:, cols] = (y_s[:, cols] * _silu(gate)).astype(_BF16)
    a = a_s[...]
    for j in range(D_MODEL // CW):
        cols = slice(j * CW, (j + 1) * CW)
        yc = _dot(a, wco_ref[:, cols])
        gc = _dot(h, wgc_ref[:, cols])
        mc_ref[:, cols] = _sigmoid(gc) * yc


def _conv_back(co, h, ln_g, ln_b, w_cgate, w_conv_out, w_gconv):
    m = co.shape[0]
    return pl.pallas_call(
        _conv_back_kernel,
        name="conv_back",
        grid=(m // TM,),
        in_specs=[
            pl.BlockSpec((TM, D_CONV), lambda i: (i, 0)),
            pl.BlockSpec((TM, D_MODEL), lambda i: (i, 0)),
            _resident((1, D_CONV)),
            _resident((1, D_CONV)),
            _resident((D_MODEL, D_CONV + W_PAD)),
            _resident((D_CONV, D_MODEL + W_PAD)),
            _resident((D_MODEL, D_MODEL + W_PAD)),
        ],
        out_specs=pl.BlockSpec((TM, D_MODEL), lambda i: (i, 0)),
        out_shape=jax.ShapeDtypeStruct((m, D_MODEL), _F32),
        scratch_shapes=[pltpu.VMEM((TM, D_CONV), _F32), pltpu.VMEM((TM, D_CONV), _BF16)],
        compiler_params=pltpu.CompilerParams(
            dimension_semantics=("arbitrary",), vmem_limit_bytes=VMEM_LIMIT),
    )(co, h, ln_g, ln_b, w_cgate, w_conv_out, w_gconv)


def _segment_scan(a, u, carry_ref, gcols, a_s, u_s, hl_s, p_s, slab0):
    row_id = lax.broadcasted_iota(jnp.int32, (SUBLANES, LANES), 0)
    outs = []
    for c in range(RNN_GROUP // LANES):
        lanes = slice(c * LANES, (c + 1) * LANES)
        clanes = slice(gcols.start + c * LANES, gcols.start + (c + 1) * LANES)
        pbase = (slab0 + c) * SCAN_SLAB
        dbase = (slab0 + c) * TM
        for sg in range(SCAN_SEGS):
            rows = slice(sg * SCAN_STEPS, (sg + 1) * SCAN_STEPS)
            dst = slice(pbase + sg * SCAN_PITCH, pbase + sg * SCAN_PITCH + SCAN_STEPS)
            a_s[dst, :] = a[rows, lanes]
            u_s[dst, :] = u[rows, lanes]
        hloc = jnp.zeros((SUBLANES, LANES), _F32)
        prod = jnp.ones((SUBLANES, LANES), _F32)
        for j in range(SCAN_STEPS):
            aj = a_s[pl.ds(pbase + j, SCAN_SEGS, stride=SCAN_PITCH), :]
            uj = u_s[pl.ds(pbase + j, SCAN_SEGS, stride=SCAN_PITCH), :]
            hloc = aj * hloc + uj
            prod = aj * prod
            hl_s[dbase + j * SUBLANES:dbase + (j + 1) * SUBLANES, :] = hloc
            p_s[dbase + j * SUBLANES:dbase + (j + 1) * SUBLANES, :] = prod
        seg_decay, seg_state = prod, hloc
        for d in (1, 2, 4):
            keep = row_id >= d
            s_sh = jnp.where(keep, pltpu.roll(seg_state, d, axis=0), 0.0)
            d_sh = jnp.where(keep, pltpu.roll(seg_decay, d, axis=0), 1.0)
            seg_state = seg_state + seg_decay * s_sh
            seg_decay = seg_decay * d_sh
        carry_in = carry_ref[:, clanes]
        h_end = seg_state + seg_decay * carry_in
        h_in = jnp.where(row_id == 0, carry_in, pltpu.roll(h_end, 1, axis=0))
        carry_ref[:, clanes] = jnp.broadcast_to(h_end[SUBLANES - 1:SUBLANES, :], (SUBLANES, LANES))
        for j in range(SCAN_STEPS):
            step = slice(dbase + j * SUBLANES, dbase + (j + 1) * SUBLANES)
            u_s[pl.ds(pbase + j, SCAN_SEGS, stride=SCAN_PITCH), :] = hl_s[step, :] + p_s[step, :] * h_in
        outs.append(jnp.concatenate(
            [u_s[pbase + sg * SCAN_PITCH:pbase + sg * SCAN_PITCH + SCAN_STEPS, :]
             for sg in range(SCAN_SEGS)], axis=0))
    return jnp.concatenate(outs, axis=1)


def _rnn_front_kernel(h_ref, wrnn_ref, rcw_ref, rcb_ref, wblk_ref, ba_ref, bx_ref, lam_ref, o_ref,
                      rbuf, a_s, u_s, hl_s, p_s, carry_s):
    s = pl.program_id(1)

    @pl.when(s == 0)
    def _():
        rbuf[0:RNN_HALO, :] = jnp.zeros((RNN_HALO, D_RNN), _F32)
        carry_s[...] = jnp.zeros((SUBLANES, D_RNN), _F32)

    h = h_ref[0]
    base = RNN_HALO - (RNN_CONV_WIDTH - 1)

    n_groups = D_RNN // RNN_GROUP
    rx_next = _dot(h, wrnn_ref[:, 0:RNN_GROUP])
    for g in range(n_groups):
        gcols = slice(g * RNN_GROUP, (g + 1) * RNN_GROUP)
        rbuf[RNN_HALO:RNN_HALO + TM, gcols] = rx_next

        v = jnp.broadcast_to(rcb_ref[:, gcols], (TM, RNN_GROUP))
        for k in range(RNN_CONV_WIDTH):
            v = v + rbuf[base + k:base + k + TM, gcols] * rcw_ref[k:k + 1, gcols]
        vb = v.astype(_BF16)

        pieces_a = [None] * 5
        pieces_x = [None] * 5
        for w in range(4):
            p = _dot(vb[:, w * LANES:w * LANES + RG_WIN], wblk_ref[4 * g + w])
            for half in range(2):
                pa = p[:, half * LANES:(half + 1) * LANES]
                px = p[:, RG_WIN + half * LANES:RG_WIN + (half + 1) * LANES]
                idx = w + half
                pieces_a[idx] = pa if pieces_a[idx] is None else pieces_a[idx] + pa
                pieces_x[idx] = px if pieces_x[idx] is None else pieces_x[idx] + px
        pre_a = jnp.concatenate(pieces_a, axis=1) + ba_ref[:, gcols]
        pre_x = jnp.concatenate(pieces_x, axis=1) + bx_ref[:, gcols]

        r_gate = _sigmoid(pre_a)
        i_gate = _sigmoid(pre_x)
        neg_lam = -lam_ref[:, gcols]
        softplus = jnp.maximum(neg_lam, 0.0) + jnp.log1p(jnp.exp(-jnp.abs(neg_lam)))
        log_a = r_gate * (-RG_C * softplus)
        a = jnp.exp(log_a)
        mult = jnp.sqrt(jnp.maximum(jnp.tanh(-log_a) * (1.0 + a * a), 0.0))
        hseq = _segment_scan(a, mult * (i_gate * v), carry_s, gcols, a_s, u_s, hl_s, p_s,
                             g * (RNN_GROUP // LANES))

        lo = (2 * g + 1) * RNN_GROUP
        if g + 1 < n_groups:
            pair = _dot(h, wrnn_ref[:, lo:lo + 2 * RNN_GROUP])
            gate, rx_next = pair[:, 0:RNN_GROUP], pair[:, RNN_GROUP:2 * RNN_GROUP]
        else:
            gate = _dot(h, wrnn_ref[:, lo:lo + RNN_GROUP])
        o_ref[0, :, gcols] = (hseq * _silu(gate)).astype(_BF16)

    rbuf[0:RNN_HALO, :] = rbuf[TM:TM + RNN_HALO, :]


def _rnn_front(h, w_rnn, rnn_conv_w, rnn_conv_b, w_blk, b_a, b_x, lam):
    bsz, seq, _ = h.shape
    return pl.pallas_call(
        _rnn_front_kernel,
        name="rnn_front",
        grid=(bsz, seq // TM),
        in_specs=[
            pl.BlockSpec((1, TM, D_MODEL), lambda b, s: (b, s, 0)),
            _resident((D_MODEL, 2 * D_RNN + W_PAD)),
            _resident((RNN_CONV_WIDTH, D_RNN)),
            _resident((1, D_RNN)),
            _resident((RNN_BLOCKS, RG_WIN, 2 * RG_WIN)),
            _resident((1, D_RNN)),
            _resident((1, D_RNN)),
            _resident((1, D_RNN)),
        ],
        out_specs=pl.BlockSpec((1, TM, D_RNN), lambda b, s: (b, s, 0)),
        out_shape=jax.ShapeDtypeStruct((bsz, seq, D_RNN), _BF16),
        scratch_shapes=[
            pltpu.VMEM((RNN_HALO + TM, D_RNN), _F32),
            pltpu.VMEM((D_RNN // LANES * SCAN_SLAB, LANES), _F32),
            pltpu.VMEM((D_RNN // LANES * SCAN_SLAB, LANES), _F32),
            pltpu.VMEM((D_RNN // LANES * TM, LANES), _F32),
            pltpu.VMEM((D_RNN // LANES * TM, LANES), _F32),
            pltpu.VMEM((SUBLANES, D_RNN), _F32),
        ],
        compiler_params=pltpu.CompilerParams(
            dimension_semantics=("arbitrary", "arbitrary"), vmem_limit_bytes=VMEM_LIMIT),
    )(h, w_rnn, rnn_conv_w, rnn_conv_b, w_blk, b_a, b_x, lam)


def _merge_out_kernel(o_ref, h_ref, mc_ref, x_ref, wro_ref, wgr_ref, wout_ref, fg_ref,
                      out_ref, m_s, res_s):
    o = o_ref[...]
    h = h_ref[...]
    for j in range(D_MODEL // CW):
        cols = slice(j * CW, (j + 1) * CW)
        yr = _dot(o, wro_ref[:, cols])
        gr = _dot(h, wgr_ref[:, cols])
        m_s[:, cols] = (mc_ref[:, cols] + _sigmoid(gr) * yr).astype(_BF16)
    merged = m_s[...]
    for j in range(D_MODEL // CW):
        cols = slice(j * CW, (j + 1) * CW)
        res_s[:, cols] = x_ref[:, cols] + _dot(merged, wout_ref[:, cols])
    res = res_s[...]
    ms = jnp.mean(res * res, axis=-1, keepdims=True)
    out_ref[...] = res * lax.rsqrt(ms + NORM_EPS) * fg_ref[...]


def _merge_out(o, h, mc, x, w_rnn_out, w_grnn, w_out, final_g):
    m = x.shape[0]
    return pl.pallas_call(
        _merge_out_kernel,
        name="merge_out",
        grid=(m // TM,),
        in_specs=[
            pl.BlockSpec((TM, D_RNN), lambda i: (i, 0)),
            pl.BlockSpec((TM, D_MODEL), lambda i: (i, 0)),
            pl.BlockSpec((TM, D_MODEL), lambda i: (i, 0)),
            pl.BlockSpec((TM, D_MODEL), lambda i: (i, 0)),
            _resident((D_RNN, D_MODEL + W_PAD)),
            _resident((D_MODEL, D_MODEL + W_PAD)),
            _resident((D_MODEL, D_MODEL + W_PAD)),
            _resident((1, D_MODEL)),
        ],
        out_specs=pl.BlockSpec((TM, D_MODEL), lambda i: (i, 0)),
        out_shape=jax.ShapeDtypeStruct((m, D_MODEL), _F32),
        scratch_shapes=[pltpu.VMEM((TM, D_MODEL), _BF16), pltpu.VMEM((TM, D_MODEL), _F32)],
        compiler_params=pltpu.CompilerParams(
            dimension_semantics=("arbitrary",), vmem_limit_bytes=VMEM_LIMIT),
    )(o, h, mc, x, w_rnn_out, w_grnn, w_out, final_g)


def _widened(w):
    return jnp.concatenate([w, w[:, :W_PAD]], axis=1).astype(_BF16)


def _window_gate_weights(w_a, w_x):
    per = RNN_GROUP // RNN_BLOCK_DIM

    def windows(w):
        w = w.reshape(RNN_BLOCKS // per, per, RNN_BLOCK_DIM, RNN_BLOCK_DIM)
        cols = []
        for j in range(per):
            off = (j * RNN_BLOCK_DIM) % LANES
            pad = (off, RG_WIN - RNN_BLOCK_DIM - off)
            cols.append(jnp.pad(w[:, j], ((0, 0), pad, pad)))
        return jnp.stack(cols, axis=1).reshape(RNN_BLOCKS, RG_WIN, RG_WIN)

    return jnp.concatenate([windows(w_a), windows(w_x)], axis=2).astype(_BF16)


def _layer(x, norm_g, w_in, conv_dw_w, conv_dw_b, conv_ln_g, conv_ln_b, w_conv_out,
           rnn_conv_w, rnn_conv_b, w_rg_a, b_rg_a, w_rg_x, b_rg_x, rg_lambda,
           w_rnn_out, w_out, out_g):
    bsz, seq, _ = x.shape
    m = bsz * seq
    row = lambda v: v.reshape(1, -1).astype(_F32)
    bounds = [0]
    for width in (D_CONV, D_CONV, D_CONV, D_RNN, D_RNN, D_MODEL, D_MODEL):
        bounds.append(bounds[-1] + width)
    seg = [w_in[:, bounds[i]:bounds[i + 1]] for i in range(7)]
    _, _, _, w_rx, w_rgate, _, w_grnn = seg
    wide = lambda i: w_in[:, bounds[i]:bounds[i + 1] + W_PAD].astype(_BF16)
    w_val, w_glu, w_cgate, w_gconv = wide(0), wide(1), wide(2), wide(5)
    group = lambda w, g: w[:, g * RNN_GROUP:(g + 1) * RNN_GROUP]
    w_rnn = jnp.concatenate([group(w, g) for g in range(D_RNN // RNN_GROUP) for w in (w_rx, w_rgate)]
                            + [w_rx[:, :W_PAD]], axis=1).astype(_BF16)

    h, co = _conv_front(x, row(norm_g), w_val, w_glu, conv_dw_w.astype(_F32), row(conv_dw_b))
    mc = _conv_back(co.reshape(m, D_CONV), h.reshape(m, D_MODEL), row(conv_ln_g), row(conv_ln_b),
                    w_cgate, _widened(w_conv_out), w_gconv)
    o = _rnn_front(h, w_rnn, rnn_conv_w.astype(_F32), row(rnn_conv_b),
                   _window_gate_weights(w_rg_a, w_rg_x), row(b_rg_a), row(b_rg_x), row(rg_lambda))
    out = _merge_out(o.reshape(m, D_RNN), h.reshape(m, D_MODEL), mc, x.reshape(m, D_MODEL),
                     _widened(w_rnn_out), _widened(w_grnn), _widened(w_out), row(out_g))
    return out.reshape(bsz, seq, D_MODEL)


def kernel(x, norm_g, w_in, conv_dw_w, conv_dw_b, conv_ln_g, conv_ln_b, w_conv_out, rnn_conv_w,
           rnn_conv_b, w_rg_a, b_rg_a, w_rg_x, b_rg_x, rg_lambda, w_rnn_out, w_out, final_norm_g):
    depth = norm_g.shape[0]
    assert depth == 1, "the final RMSNorm is fused into the single layer's last call"
    return _layer(x, norm_g[0], w_in[0], conv_dw_w[0], conv_dw_b[0], conv_ln_g[0], conv_ln_b[0],
                  w_conv_out[0], rnn_conv_w[0], rnn_conv_b[0], w_rg_a[0], b_rg_a[0], w_rg_x[0],
                  b_rg_x[0], rg_lambda[0], w_rnn_out[0], w_out[0], final_norm_g)
```

```python
import functools

import jax
import jax.numpy as jnp
from jax import lax
from jax.experimental import pallas as pl
from jax.experimental.pallas import tpu as pltpu

D_MODEL = 2048
D_CONV = 2048
CONV_WIDTH = 31
RNN_BLOCKS = 16
RNN_BLOCK_DIM = 160
D_RNN = RNN_BLOCKS * RNN_BLOCK_DIM
RNN_CONV_WIDTH = 4
RG_C = 8.0
NORM_EPS = 1e-6
LN_EPS = 1e-5

LANES = 128
SUBLANES = 8
VMEM_LIMIT = 56 * 1024 * 1024

TM = 256
TMC = 512
CW = 512
PW = 256
CONV_HALO = 32
CONV_ROWS = 32
SH_ROWS = TMC + CONV_HALO - SUBLANES
RNN_HALO = 8
RNN_GROUP = 4 * RNN_BLOCK_DIM
RG_WIN = 2 * LANES
SCAN_SEGS = SUBLANES
SCAN_STEPS = TM // SCAN_SEGS
SCAN_PITCH = SCAN_STEPS + 4
SCAN_SLAB = SCAN_SEGS * SCAN_PITCH
W_PAD = LANES

_F32 = jnp.float32
_BF16 = jnp.bfloat16


def _dot(a, b):
    return jnp.dot(a, b, preferred_element_type=_F32)


def _sigmoid(x):
    return jax.nn.sigmoid(x)


def _silu(x):
    return x * jax.nn.sigmoid(x)


def _resident(shape):
    nd = len(shape)
    return pl.BlockSpec(shape, lambda *_: (0,) * nd, pipeline_mode=pl.Buffered(1))


def _conv_front_kernel(x_ref, ng_ref, wv_ref, wg_ref, cw_ref, cb_ref, h_ref, co_ref, sh, halo):
    s = pl.program_id(1)
    n_pieces = D_CONV // PW

    @pl.when(s == 0)
    def _():
        halo[...] = jnp.zeros(halo.shape, _F32)

    x = x_ref[0]
    ms = jnp.mean(x * x, axis=-1, keepdims=True)
    h = (x * lax.rsqrt(ms + NORM_EPS) * ng_ref[...]).astype(_BF16)
    h_ref[0] = h

    base = CONV_HALO - (CONV_WIDTH - 1)

    for p in range(n_pieces):
        cols = slice(p * PW, (p + 1) * PW)
        val = _dot(h, wv_ref[:, cols])
        glu = _dot(h, wg_ref[:, cols])
        sh[0, 0:CONV_HALO, :] = halo[p]
        sh[0, CONV_HALO:CONV_HALO + TMC, :] = val * _sigmoid(glu)
        halo[p] = sh[0, TMC:TMC + CONV_HALO, :]
        for r in range(1, SUBLANES):
            sh[r, 0:SH_ROWS, :] = sh[0, r:r + SH_ROWS, :]
        for half in range(PW // LANES):
            lane = slice(half * LANES, (half + 1) * LANES)
            out_lane = slice(p * PW + half * LANES, p * PW + (half + 1) * LANES)
            for r0 in range(0, TMC, CONV_ROWS):
                acc = jnp.broadcast_to(cb_ref[:, out_lane], (CONV_ROWS, LANES))
                for k in range(CONV_WIDTH):
                    r = (k + base) % SUBLANES
                    q = r0 + k + base - r
                    acc = acc + sh[r, q:q + CONV_ROWS, lane] * cw_ref[k:k + 1, out_lane]
                co_ref[0, r0:r0 + CONV_ROWS, out_lane] = acc


def _conv_front(x, norm_g, w_val, w_glu, conv_w, conv_b):
    bsz, seq, _ = x.shape
    return pl.pallas_call(
        _conv_front_kernel,
        name="conv_front",
        grid=(bsz, seq // TMC),
        in_specs=[
            pl.BlockSpec((1, TMC, D_MODEL), lambda b, s: (b, s, 0)),
            _resident((1, D_MODEL)),
            _resident((D_MODEL, D_CONV + W_PAD)),
            _resident((D_MODEL, D_CONV + W_PAD)),
            _resident((CONV_WIDTH, D_CONV)),
            _resident((1, D_CONV)),
        ],
        out_specs=[
            pl.BlockSpec((1, TMC, D_MODEL), lambda b, s: (b, s, 0)),
            pl.BlockSpec((1, TMC, D_CONV), lambda b, s: (b, s, 0)),
        ],
        out_shape=[
            jax.ShapeDtypeStruct((bsz, seq, D_MODEL), _BF16),
            jax.ShapeDtypeStruct((bsz, seq, D_CONV), _F32),
        ],
        scratch_shapes=[pltpu.VMEM((SUBLANES, CONV_HALO + TMC, PW), _F32),
                        pltpu.VMEM((D_CONV // PW, CONV_HALO, PW), _F32)],
        compiler_params=pltpu.CompilerParams(
            dimension_semantics=("arbitrary", "arbitrary"), vmem_limit_bytes=VMEM_LIMIT),
    )(x, norm_g, w_val, w_glu, conv_w, conv_b)


def _conv_back_kernel(co_ref, h_ref, lg_ref, lb_ref, wcg_ref, wco_ref, wgc_ref, mc_ref, y_s, a_s):
    co = co_ref[...]
    mu = jnp.mean(co, axis=-1, keepdims=True)
    xc = co - mu
    var = jnp.mean(xc * xc, axis=-1, keepdims=True)
    y = xc * lax.rsqrt(var + LN_EPS) * lg_ref[...] + lb_ref[...]
    y_s[...] = _silu(y)
    h = h_ref[...]
    for j in range(D_CONV // CW):
        cols = slice(j * CW, (j + 1) * CW)
        gate = _dot(h, wcg_ref[:, cols])
        a_s[:, cols] = (y_s[:, cols] * _silu(gate)).astype(_BF16)
    a = a_s[...]
    for j in range(D_MODEL // CW):
        cols = slice(j * CW, (j + 1) * CW)
        yc = _dot(a, wco_ref[:, cols])
        gc = _dot(h, wgc_ref[:, cols])
        mc_ref[:, cols] = _sigmoid(gc) * yc


def _conv_back(co, h, ln_g, ln_b, w_cgate, w_conv_out, w_gconv):
    m = co.shape[0]
    return pl.pallas_call(
        _conv_back_kernel,
        name="conv_back",
        grid=(m // TM,),
        in_specs=[
            pl.BlockSpec((TM, D_CONV), lambda i: (i, 0)),
            pl.BlockSpec((TM, D_MODEL), lambda i: (i, 0)),
            _resident((1, D_CONV)),
            _resident((1, D_CONV)),
            _resident((D_MODEL, D_CONV + W_PAD)),
            _resident((D_CONV, D_MODEL + W_PAD)),
            _resident((D_MODEL, D_MODEL + W_PAD)),
        ],
        out_specs=pl.BlockSpec((TM, D_MODEL), lambda i: (i, 0)),
        out_shape=jax.ShapeDtypeStruct((m, D_MODEL), _F32),
        scratch_shapes=[pltpu.VMEM((TM, D_CONV), _F32), pltpu.VMEM((TM, D_CONV), _BF16)],
        compiler_params=pltpu.CompilerParams(
            dimension_semantics=("arbitrary",), vmem_limit_bytes=VMEM_LIMIT),
    )(co, h, ln_g, ln_b, w_cgate, w_conv_out, w_gconv)


def _segment_scan(a, u, carry_ref, gcols, a_s, u_s, hl_s, p_s, slab0):
    row_id = lax.broadcasted_iota(jnp.int32, (SUBLANES, LANES), 0)
    outs = []
    for c in range(RNN_GROUP // LANES):
        lanes = slice(c * LANES, (c + 1) * LANES)
        clanes = slice(gcols.start + c * LANES, gcols.start + (c + 1) * LANES)
        pbase = (slab0 + c) * SCAN_SLAB
        dbase = (slab0 + c) * TM
        for sg in range(SCAN_SEGS):
            rows = slice(sg * SCAN_STEPS, (sg + 1) * SCAN_STEPS)
            dst = slice(pbase + sg * SCAN_PITCH, pbase + sg * SCAN_PITCH + SCAN_STEPS)
            a_s[dst, :] = a[rows, lanes]
            u_s[dst, :] = u[rows, lanes]
        hloc = jnp.zeros((SUBLANES, LANES), _F32)
        prod = jnp.ones((SUBLANES, LANES), _F32)
        for j in range(SCAN_STEPS):
            aj = a_s[pl.ds(pbase + j, SCAN_SEGS, stride=SCAN_PITCH), :]
            uj = u_s[pl.ds(pbase + j, SCAN_SEGS, stride=SCAN_PITCH), :]
            hloc = aj * hloc + uj
            prod = aj * prod
            hl_s[dbase + j * SUBLANES:dbase + (j + 1) * SUBLANES, :] = hloc
            p_s[dbase + j * SUBLANES:dbase + (j + 1) * SUBLANES, :] = prod
        seg_decay, seg_state = prod, hloc
        for d in (1, 2, 4):
            keep = row_id >= d
            s_sh = jnp.where(keep, pltpu.roll(seg_state, d, axis=0), 0.0)
            d_sh = jnp.where(keep, pltpu.roll(seg_decay, d, axis=0), 1.0)
            seg_state = seg_state + seg_decay * s_sh
            seg_decay = seg_decay * d_sh
        carry_in = carry_ref[:, clanes]
        h_end = seg_state + seg_decay * carry_in
        h_in = jnp.where(row_id == 0, carry_in, pltpu.roll(h_end, 1, axis=0))
        carry_ref[:, clanes] = jnp.broadcast_to(h_end[SUBLANES - 1:SUBLANES, :], (SUBLANES, LANES))
        for j in range(SCAN_STEPS):
            step = slice(dbase + j * SUBLANES, dbase + (j + 1) * SUBLANES)
            u_s[pl.ds(pbase + j, SCAN_SEGS, stride=SCAN_PITCH), :] = hl_s[step, :] + p_s[step, :] * h_in
        outs.append(jnp.concatenate(
            [u_s[pbase + sg * SCAN_PITCH:pbase + sg * SCAN_PITCH + SCAN_STEPS, :]
             for sg in range(SCAN_SEGS)], axis=0))
    return jnp.concatenate(outs, axis=1)


def _rnn_front_kernel(h_ref, wrnn_ref, rcw_ref, rcb_ref, wblk_ref, ba_ref, bx_ref, lam_ref, o_ref,
                      rbuf, a_s, u_s, hl_s, p_s, carry_s):
    s = pl.program_id(1)

    @pl.when(s == 0)
    def _():
        rbuf[0:RNN_HALO, :] = jnp.zeros((RNN_HALO, D_RNN), _F32)
        carry_s[...] = jnp.zeros((SUBLANES, D_RNN), _F32)

    h = h_ref[0]
    base = RNN_HALO - (RNN_CONV_WIDTH - 1)

    n_groups = D_RNN // RNN_GROUP
    rx_next = _dot(h, wrnn_ref[:, 0:RNN_GROUP])
    for g in range(n_groups):
        gcols = slice(g * RNN_GROUP, (g + 1) * RNN_GROUP)
        rbuf[RNN_HALO:RNN_HALO + TM, gcols] = rx_next

        win = rbuf[:, gcols]
        v = jnp.broadcast_to(rcb_ref[:, gcols], (TM, RNN_GROUP))
        for k in range(RNN_CONV_WIDTH):
            r = (base + k) % SUBLANES
            q = base + k - r
            shifted = win if r == 0 else pltpu.roll(win, win.shape[0] - r, axis=0)
            v = v + shifted[q:q + TM] * rcw_ref[k:k + 1, gcols]
        vb = v.astype(_BF16)

        pieces_a = [None] * 5
        pieces_x = [None] * 5
        for w in range(4):
            p = _dot(vb[:, w * LANES:w * LANES + RG_WIN], wblk_ref[4 * g + w])
            for half in range(2):
                pa = p[:, half * LANES:(half + 1) * LANES]
                px = p[:, RG_WIN + half * LANES:RG_WIN + (half + 1) * LANES]
                idx = w + half
                pieces_a[idx] = pa if pieces_a[idx] is None else pieces_a[idx] + pa
                pieces_x[idx] = px if pieces_x[idx] is None else pieces_x[idx] + px
        pre_a = jnp.concatenate(pieces_a, axis=1) + ba_ref[:, gcols]
        pre_x = jnp.concatenate(pieces_x, axis=1) + bx_ref[:, gcols]

        r_gate = _sigmoid(pre_a)
        i_gate = _sigmoid(pre_x)
        neg_lam = -lam_ref[:, gcols]
        softplus = jnp.maximum(neg_lam, 0.0) + jnp.log1p(jnp.exp(-jnp.abs(neg_lam)))
        log_a = r_gate * (-RG_C * softplus)
        a = jnp.exp(log_a)
        mult = jnp.sqrt(jnp.maximum(jnp.tanh(-log_a) * (1.0 + a * a), 0.0))
        hseq = _segment_scan(a, mult * (i_gate * v), carry_s, gcols, a_s, u_s, hl_s, p_s,
                             g * (RNN_GROUP // LANES))

        lo = (2 * g + 1) * RNN_GROUP
        if g + 1 < n_groups:
            pair = _dot(h, wrnn_ref[:, lo:lo + 2 * RNN_GROUP])
            gate, rx_next = pair[:, 0:RNN_GROUP], pair[:, RNN_GROUP:2 * RNN_GROUP]
        else:
            gate = _dot(h, wrnn_ref[:, lo:lo + RNN_GROUP])
        o_ref[0, :, gcols] = (hseq * _silu(gate)).astype(_BF16)

    rbuf[0:RNN_HALO, :] = rbuf[TM:TM + RNN_HALO, :]


def _rnn_front(h, w_rnn, rnn_conv_w, rnn_conv_b, w_blk, b_a, b_x, lam):
    bsz, seq, _ = h.shape
    return pl.pallas_call(
        _rnn_front_kernel,
        name="rnn_front",
        grid=(bsz, seq // TM),
        in_specs=[
            pl.BlockSpec((1, TM, D_MODEL), lambda b, s: (b, s, 0)),
            _resident((D_MODEL, 2 * D_RNN + W_PAD)),
            _resident((RNN_CONV_WIDTH, D_RNN)),
            _resident((1, D_RNN)),
            _resident((RNN_BLOCKS, RG_WIN, 2 * RG_WIN)),
            _resident((1, D_RNN)),
            _resident((1, D_RNN)),
            _resident((1, D_RNN)),
        ],
        out_specs=pl.BlockSpec((1, TM, D_RNN), lambda b, s: (b, s, 0)),
        out_shape=jax.ShapeDtypeStruct((bsz, seq, D_RNN), _BF16),
        scratch_shapes=[
            pltpu.VMEM((RNN_HALO + TM, D_RNN), _F32),
            pltpu.VMEM((D_RNN // LANES * SCAN_SLAB, LANES), _F32),
            pltpu.VMEM((D_RNN // LANES * SCAN_SLAB, LANES), _F32),
            pltpu.VMEM((D_RNN // LANES * TM, LANES), _F32),
            pltpu.VMEM((D_RNN // LANES * TM, LANES), _F32),
            pltpu.VMEM((SUBLANES, D_RNN), _F32),
        ],
        compiler_params=pltpu.CompilerParams(
            dimension_semantics=("arbitrary", "arbitrary"), vmem_limit_bytes=VMEM_LIMIT),
    )(h, w_rnn, rnn_conv_w, rnn_conv_b, w_blk, b_a, b_x, lam)


def _merge_out_kernel(o_ref, h_ref, mc_ref, x_ref, wro_ref, wgr_ref, wout_ref, fg_ref,
                      out_ref, m_s, res_s):
    o = o_ref[...]
    h = h_ref[...]
    for j in range(D_MODEL // CW):
        cols = slice(j * CW, (j + 1) * CW)
        yr = _dot(o, wro_ref[:, cols])
        gr = _dot(h, wgr_ref[:, cols])
        m_s[:, cols] = (mc_ref[:, cols] + _sigmoid(gr) * yr).astype(_BF16)
    merged = m_s[...]
    for j in range(D_MODEL // CW):
        cols = slice(j * CW, (j + 1) * CW)
        res_s[:, cols] = x_ref[:, cols] + _dot(merged, wout_ref[:, cols])
    res = res_s[...]
    ms = jnp.mean(res * res, axis=-1, keepdims=True)
    out_ref[...] = res * lax.rsqrt(ms + NORM_EPS) * fg_ref[...]


def _merge_out(o, h, mc, x, w_rnn_out, w_grnn, w_out, final_g):
    m = x.shape[0]
    return pl.pallas_call(
        _merge_out_kernel,
        name="merge_out",
        grid=(m // TM,),
        in_specs=[
            pl.BlockSpec((TM, D_RNN), lambda i: (i, 0)),
            pl.BlockSpec((TM, D_MODEL), lambda i: (i, 0)),
            pl.BlockSpec((TM, D_MODEL), lambda i: (i, 0)),
            pl.BlockSpec((TM, D_MODEL), lambda i: (i, 0)),
            _resident((D_RNN, D_MODEL + W_PAD)),
            _resident((D_MODEL, D_MODEL + W_PAD)),
            _resident((D_MODEL, D_MODEL + W_PAD)),
            _resident((1, D_MODEL)),
        ],
        out_specs=pl.BlockSpec((TM, D_MODEL), lambda i: (i, 0)),
        out_shape=jax.ShapeDtypeStruct((m, D_MODEL), _F32),
        scratch_shapes=[pltpu.VMEM((TM, D_MODEL), _BF16), pltpu.VMEM((TM, D_MODEL), _F32)],
        compiler_params=pltpu.CompilerParams(
            dimension_semantics=("arbitrary",), vmem_limit_bytes=VMEM_LIMIT),
    )(o, h, mc, x, w_rnn_out, w_grnn, w_out, final_g)


def _widened(w):
    return jnp.concatenate([w, w[:, :W_PAD]], axis=1).astype(_BF16)


def _window_gate_weights(w_a, w_x):
    per = RNN_GROUP // RNN_BLOCK_DIM

    def windows(w):
        w = w.reshape(RNN_BLOCKS // per, per, RNN_BLOCK_DIM, RNN_BLOCK_DIM)
        cols = []
        for j in range(per):
            off = (j * RNN_BLOCK_DIM) % LANES
            pad = (off, RG_WIN - RNN_BLOCK_DIM - off)
            cols.append(jnp.pad(w[:, j], ((0, 0), pad, pad)))
        return jnp.stack(cols, axis=1).reshape(RNN_BLOCKS, RG_WIN, RG_WIN)

    return jnp.concatenate([windows(w_a), windows(w_x)], axis=2).astype(_BF16)


def _layer(x, norm_g, w_in, conv_dw_w, conv_dw_b, conv_ln_g, conv_ln_b, w_conv_out,
           rnn_conv_w, rnn_conv_b, w_rg_a, b_rg_a, w_rg_x, b_rg_x, rg_lambda,
           w_rnn_out, w_out, out_g):
    bsz, seq, _ = x.shape
    m = bsz * seq
    row = lambda v: v.reshape(1, -1).astype(_F32)
    bounds = [0]
    for width in (D_CONV, D_CONV, D_CONV, D_RNN, D_RNN, D_MODEL, D_MODEL):
        bounds.append(bounds[-1] + width)
    seg = [w_in[:, bounds[i]:bounds[i + 1]] for i in range(7)]
    _, _, _, w_rx, w_rgate, _, w_grnn = seg
    wide = lambda i: w_in[:, bounds[i]:bounds[i + 1] + W_PAD].astype(_BF16)
    w_val, w_glu, w_cgate, w_gconv = wide(0), wide(1), wide(2), wide(5)
    group = lambda w, g: w[:, g * RNN_GROUP:(g + 1) * RNN_GROUP]
    w_rnn = jnp.concatenate([group(w, g) for g in range(D_RNN // RNN_GROUP) for w in (w_rx, w_rgate)]
                            + [w_rx[:, :W_PAD]], axis=1).astype(_BF16)

    h, co = _conv_front(x, row(norm_g), w_val, w_glu, conv_dw_w.astype(_F32), row(conv_dw_b))
    mc = _conv_back(co.reshape(m, D_CONV), h.reshape(m, D_MODEL), row(conv_ln_g), row(conv_ln_b),
                    w_cgate, _widened(w_conv_out), w_gconv)
    o = _rnn_front(h, w_rnn, rnn_conv_w.astype(_F32), row(rnn_conv_b),
                   _window_gate_weights(w_rg_a, w_rg_x), row(b_rg_a), row(b_rg_x), row(rg_lambda))
    out = _merge_out(o.reshape(m, D_RNN), h.reshape(m, D_MODEL), mc, x.reshape(m, D_MODEL),
                     _widened(w_rnn_out), _widened(w_grnn), _widened(w_out), row(out_g))
    return out.reshape(bsz, seq, D_MODEL)


def kernel(x, norm_g, w_in, conv_dw_w, conv_dw_b, conv_ln_g, conv_ln_b, w_conv_out, rnn_conv_w,
           rnn_conv_b, w_rg_a, b_rg_a, w_rg_x, b_rg_x, rg_lambda, w_rnn_out, w_out, final_norm_g):
    depth = norm_g.shape[0]
    assert depth == 1, "the final RMSNorm is fused into the single layer's last call"
    return _layer(x, norm_g[0], w_in[0], conv_dw_w[0], conv_dw_b[0], conv_ln_g[0], conv_ln_b[0],
                  w_conv_out[0], rnn_conv_w[0], rnn_conv_b[0], w_rg_a[0], b_rg_a[0], w_rg_x[0],
                  b_rg_x[0], rg_lambda[0], w_rnn_out[0], w_out[0], final_norm_g)
```

```python
import functools

import jax
import jax.numpy as jnp
from jax import lax
from jax.experimental import pallas as pl
from jax.experimental.pallas import tpu as pltpu

D_MODEL = 2048
D_CONV = 2048
CONV_WIDTH = 31
RNN_BLOCKS = 16
RNN_BLOCK_DIM = 160
D_RNN = RNN_BLOCKS * RNN_BLOCK_DIM
RNN_CONV_WIDTH = 4
RG_C = 8.0
NORM_EPS = 1e-6
LN_EPS = 1e-5

LANES = 128
SUBLANES = 8
VMEM_LIMIT = 56 * 1024 * 1024

TM = 256
TMC = 512
TMR = 2 * TM
CW = 512
PW = 256
CONV_HALO = 32
CONV_ROWS = 32
SH_ROWS = TMC + CONV_HALO - SUBLANES
RNN_HALO = 8
RNN_GROUP = 4 * RNN_BLOCK_DIM
RG_WIN = 2 * LANES
SCAN_SEGS = SUBLANES
SCAN_STEPS = TM // SCAN_SEGS
SCAN_PITCH = SCAN_STEPS + 4
SCAN_SLAB = SCAN_SEGS * SCAN_PITCH
W_PAD = LANES

_F32 = jnp.float32
_BF16 = jnp.bfloat16


def _dot(a, b):
    return jnp.dot(a, b, preferred_element_type=_F32)


def _sigmoid(x):
    return jax.nn.sigmoid(x)


def _silu(x):
    return x * jax.nn.sigmoid(x)


def _resident(shape):
    nd = len(shape)
    return pl.BlockSpec(shape, lambda *_: (0,) * nd, pipeline_mode=pl.Buffered(1))


def _conv_front_kernel(x_ref, ng_ref, wv_ref, wg_ref, cw_ref, cb_ref, h_ref, co_ref, sh, halo):
    s = pl.program_id(1)
    n_pieces = D_CONV // PW

    @pl.when(s == 0)
    def _():
        halo[...] = jnp.zeros(halo.shape, _F32)

    x = x_ref[0]
    ms = jnp.mean(x * x, axis=-1, keepdims=True)
    h = (x * lax.rsqrt(ms + NORM_EPS) * ng_ref[...]).astype(_BF16)
    h_ref[0] = h

    base = CONV_HALO - (CONV_WIDTH - 1)

    for p in range(n_pieces):
        cols = slice(p * PW, (p + 1) * PW)
        val = _dot(h, wv_ref[:, cols])
        glu = _dot(h, wg_ref[:, cols])
        sh[0, 0:CONV_HALO, :] = halo[p]
        sh[0, CONV_HALO:CONV_HALO + TMC, :] = val * _sigmoid(glu)
        halo[p] = sh[0, TMC:TMC + CONV_HALO, :]
        for r in range(1, SUBLANES):
            sh[r, 0:SH_ROWS, :] = sh[0, r:r + SH_ROWS, :]
        for half in range(PW // LANES):
            lane = slice(half * LANES, (half + 1) * LANES)
            out_lane = slice(p * PW + half * LANES, p * PW + (half + 1) * LANES)
            for r0 in range(0, TMC, CONV_ROWS):
                acc = jnp.broadcast_to(cb_ref[:, out_lane], (CONV_ROWS, LANES))
                for k in range(CONV_WIDTH):
                    r = (k + base) % SUBLANES
                    q = r0 + k + base - r
                    acc = acc + sh[r, q:q + CONV_ROWS, lane] * cw_ref[k:k + 1, out_lane]
                co_ref[0, r0:r0 + CONV_ROWS, out_lane] = acc


def _conv_front(x, norm_g, w_val, w_glu, conv_w, conv_b):
    bsz, seq, _ = x.shape
    return pl.pallas_call(
        _conv_front_kernel,
        name="conv_front",
        grid=(bsz, seq // TMC),
        in_specs=[
            pl.BlockSpec((1, TMC, D_MODEL), lambda b, s: (b, s, 0)),
            _resident((1, D_MODEL)),
            _resident((D_MODEL, D_CONV + W_PAD)),
            _resident((D_MODEL, D_CONV + W_PAD)),
            _resident((CONV_WIDTH, D_CONV)),
            _resident((1, D_CONV)),
        ],
        out_specs=[
            pl.BlockSpec((1, TMC, D_MODEL), lambda b, s: (b, s, 0)),
            pl.BlockSpec((1, TMC, D_CONV), lambda b, s: (b, s, 0)),
        ],
        out_shape=[
            jax.ShapeDtypeStruct((bsz, seq, D_MODEL), _BF16),
            jax.ShapeDtypeStruct((bsz, seq, D_CONV), _F32),
        ],
        scratch_shapes=[pltpu.VMEM((SUBLANES, CONV_HALO + TMC, PW), _F32),
                        pltpu.VMEM((D_CONV // PW, CONV_HALO, PW), _F32)],
        compiler_params=pltpu.CompilerParams(
            dimension_semantics=("arbitrary", "arbitrary"), vmem_limit_bytes=VMEM_LIMIT),
    )(x, norm_g, w_val, w_glu, conv_w, conv_b)


def _conv_back_kernel(co_ref, h_ref, lg_ref, lb_ref, wcg_ref, wco_ref, wgc_ref, mc_ref, y_s, a_s):
    co = co_ref[...]
    mu = jnp.mean(co, axis=-1, keepdims=True)
    xc = co - mu
    var = jnp.mean(xc * xc, axis=-1, keepdims=True)
    y = xc * lax.rsqrt(var + LN_EPS) * lg_ref[...] + lb_ref[...]
    y_s[...] = _silu(y)
    h = h_ref[...]
    for j in range(D_CONV // CW):
        cols = slice(j * CW, (j + 1) * CW)
        gate = _dot(h, wcg_ref[:, cols])
        a_s[:, cols] = (y_s[:, cols] * _silu(gate)).astype(_BF16)
    a = a_s[...]
    for j in range(D_MODEL // CW):
        cols = slice(j * CW, (j + 1) * CW)
        yc = _dot(a, wco_ref[:, cols])
        gc = _dot(h, wgc_ref[:, cols])
        mc_ref[:, cols] = _sigmoid(gc) * yc


def _conv_back(co, h, ln_g, ln_b, w_cgate, w_conv_out, w_gconv):
    m = co.shape[0]
    return pl.pallas_call(
        _conv_back_kernel,
        name="conv_back",
        grid=(m // TM,),
        in_specs=[
            pl.BlockSpec((TM, D_CONV), lambda i: (i, 0)),
            pl.BlockSpec((TM, D_MODEL), lambda i: (i, 0)),
            _resident((1, D_CONV)),
            _resident((1, D_CONV)),
            _resident((D_MODEL, D_CONV + W_PAD)),
            _resident((D_CONV, D_MODEL + W_PAD)),
            _resident((D_MODEL, D_MODEL + W_PAD)),
        ],
        out_specs=pl.BlockSpec((TM, D_MODEL), lambda i: (i, 0)),
        out_shape=jax.ShapeDtypeStruct((m, D_MODEL), _F32),
        scratch_shapes=[pltpu.VMEM((TM, D_CONV), _F32), pltpu.VMEM((TM, D_CONV), _BF16)],
        compiler_params=pltpu.CompilerParams(
            dimension_semantics=("arbitrary",), vmem_limit_bytes=VMEM_LIMIT),
    )(co, h, ln_g, ln_b, w_cgate, w_conv_out, w_gconv)


def _segment_scan(a, u, carry_ref, gcols, a_s, u_s, hl_s, p_s, slab0):
    row_id = lax.broadcasted_iota(jnp.int32, (SUBLANES, LANES), 0)
    outs = []
    for c in range(RNN_GROUP // LANES):
        lanes = slice(c * LANES, (c + 1) * LANES)
        clanes = slice(gcols.start + c * LANES, gcols.start + (c + 1) * LANES)
        pbase = (slab0 + c) * SCAN_SLAB
        dbase = (slab0 + c) * TM
        for sg in range(SCAN_SEGS):
            rows = slice(sg * SCAN_STEPS, (sg + 1) * SCAN_STEPS)
            dst = slice(pbase + sg * SCAN_PITCH, pbase + sg * SCAN_PITCH + SCAN_STEPS)
            a_s[dst, :] = a[rows, lanes]
            u_s[dst, :] = u[rows, lanes]
        hloc = jnp.zeros((SUBLANES, LANES), _F32)
        prod = jnp.ones((SUBLANES, LANES), _F32)
        for j in range(SCAN_STEPS):
            aj = a_s[pl.ds(pbase + j, SCAN_SEGS, stride=SCAN_PITCH), :]
            uj = u_s[pl.ds(pbase + j, SCAN_SEGS, stride=SCAN_PITCH), :]
            hloc = aj * hloc + uj
            prod = aj * prod
            hl_s[dbase + j * SUBLANES:dbase + (j + 1) * SUBLANES, :] = hloc
            p_s[dbase + j * SUBLANES:dbase + (j + 1) * SUBLANES, :] = prod
        seg_decay, seg_state = prod, hloc
        for d in (1, 2, 4):
            keep = row_id >= d
            s_sh = jnp.where(keep, pltpu.roll(seg_state, d, axis=0), 0.0)
            d_sh = jnp.where(keep, pltpu.roll(seg_decay, d, axis=0), 1.0)
            seg_state = seg_state + seg_decay * s_sh
            seg_decay = seg_decay * d_sh
        carry_in = carry_ref[:, clanes]
        h_end = seg_state + seg_decay * carry_in
        h_in = jnp.where(row_id == 0, carry_in, pltpu.roll(h_end, 1, axis=0))
        carry_ref[:, clanes] = jnp.broadcast_to(h_end[SUBLANES - 1:SUBLANES, :], (SUBLANES, LANES))
        for j in range(SCAN_STEPS):
            step = slice(dbase + j * SUBLANES, dbase + (j + 1) * SUBLANES)
            u_s[pl.ds(pbase + j, SCAN_SEGS, stride=SCAN_PITCH), :] = hl_s[step, :] + p_s[step, :] * h_in
        outs.append(jnp.concatenate(
            [u_s[pbase + sg * SCAN_PITCH:pbase + sg * SCAN_PITCH + SCAN_STEPS, :]
             for sg in range(SCAN_SEGS)], axis=0))
    return jnp.concatenate(outs, axis=1)


def _rnn_front_kernel(h_ref, wrnn_ref, rcw_ref, rcb_ref, wblk_ref, ba_ref, bx_ref, lam_ref, o_ref,
                      rbuf, a_s, u_s, hl_s, p_s, carry_s):
    s = pl.program_id(1)

    @pl.when(s == 0)
    def _():
        rbuf[0:RNN_HALO, :] = jnp.zeros((RNN_HALO, D_RNN), _F32)
        carry_s[...] = jnp.zeros((SUBLANES, D_RNN), _F32)

    for lo in range(0, TMR, TM):
        _rnn_rows(lo, h_ref, wrnn_ref, rcw_ref, rcb_ref, wblk_ref, ba_ref, bx_ref, lam_ref, o_ref,
                  rbuf, a_s, u_s, hl_s, p_s, carry_s)
    rbuf[0:RNN_HALO, :] = rbuf[TMR:TMR + RNN_HALO, :]


def _rnn_rows(lo, h_ref, wrnn_ref, rcw_ref, rcb_ref, wblk_ref, ba_ref, bx_ref, lam_ref, o_ref,
              rbuf, a_s, u_s, hl_s, p_s, carry_s):
    h = h_ref[0, lo:lo + TM]
    base = RNN_HALO - (RNN_CONV_WIDTH - 1)

    n_groups = D_RNN // RNN_GROUP
    rx_next = _dot(h, wrnn_ref[:, 0:RNN_GROUP])
    for g in range(n_groups):
        gcols = slice(g * RNN_GROUP, (g + 1) * RNN_GROUP)
        rbuf[RNN_HALO + lo:RNN_HALO + lo + TM, gcols] = rx_next

        win = rbuf[lo:lo + RNN_HALO + TM, gcols]
        v = jnp.broadcast_to(rcb_ref[:, gcols], (TM, RNN_GROUP))
        for k in range(RNN_CONV_WIDTH):
            r = (base + k) % SUBLANES
            q = base + k - r
            shifted = win if r == 0 else pltpu.roll(win, win.shape[0] - r, axis=0)
            v = v + shifted[q:q + TM] * rcw_ref[k:k + 1, gcols]
        vb = v.astype(_BF16)

        pieces_a = [None] * 5
        pieces_x = [None] * 5
        for w in range(4):
            p = _dot(vb[:, w * LANES:w * LANES + RG_WIN], wblk_ref[4 * g + w])
            for half in range(2):
                pa = p[:, half * LANES:(half + 1) * LANES]
                px = p[:, RG_WIN + half * LANES:RG_WIN + (half + 1) * LANES]
                idx = w + half
                pieces_a[idx] = pa if pieces_a[idx] is None else pieces_a[idx] + pa
                pieces_x[idx] = px if pieces_x[idx] is None else pieces_x[idx] + px
        pre_a = jnp.concatenate(pieces_a, axis=1) + ba_ref[:, gcols]
        pre_x = jnp.concatenate(pieces_x, axis=1) + bx_ref[:, gcols]

        r_gate = _sigmoid(pre_a)
        i_gate = _sigmoid(pre_x)
        neg_lam = -lam_ref[:, gcols]
        softplus = jnp.maximum(neg_lam, 0.0) + jnp.log1p(jnp.exp(-jnp.abs(neg_lam)))
        log_a = r_gate * (-RG_C * softplus)
        a = jnp.exp(log_a)
        mult = jnp.sqrt(jnp.maximum(jnp.tanh(-log_a) * (1.0 + a * a), 0.0))
        hseq = _segment_scan(a, mult * (i_gate * v), carry_s, gcols, a_s, u_s, hl_s, p_s,
                             g * (RNN_GROUP // LANES))

        wcol = (2 * g + 1) * RNN_GROUP
        if g + 1 < n_groups:
            pair = _dot(h, wrnn_ref[:, wcol:wcol + 2 * RNN_GROUP])
            gate, rx_next = pair[:, 0:RNN_GROUP], pair[:, RNN_GROUP:2 * RNN_GROUP]
        else:
            gate = _dot(h, wrnn_ref[:, wcol:wcol + RNN_GROUP])
        o_ref[0, lo:lo + TM, gcols] = (hseq * _silu(gate)).astype(_BF16)


def _rnn_front(h, w_rnn, rnn_conv_w, rnn_conv_b, w_blk, b_a, b_x, lam):
    bsz, seq, _ = h.shape
    return pl.pallas_call(
        _rnn_front_kernel,
        name="rnn_front",
        grid=(bsz, seq // TMR),
        in_specs=[
            pl.BlockSpec((1, TMR, D_MODEL), lambda b, s: (b, s, 0)),
            _resident((D_MODEL, 2 * D_RNN + W_PAD)),
            _resident((RNN_CONV_WIDTH, D_RNN)),
            _resident((1, D_RNN)),
            _resident((RNN_BLOCKS, RG_WIN, 2 * RG_WIN)),
            _resident((1, D_RNN)),
            _resident((1, D_RNN)),
            _resident((1, D_RNN)),
        ],
        out_specs=pl.BlockSpec((1, TMR, D_RNN), lambda b, s: (b, s, 0)),
        out_shape=jax.ShapeDtypeStruct((bsz, seq, D_RNN), _BF16),
        scratch_shapes=[
            pltpu.VMEM((RNN_HALO + TMR, D_RNN), _F32),
            pltpu.VMEM((D_RNN // LANES * SCAN_SLAB, LANES), _F32),
            pltpu.VMEM((D_RNN // LANES * SCAN_SLAB, LANES), _F32),
            pltpu.VMEM((D_RNN // LANES * TM, LANES), _F32),
            pltpu.VMEM((D_RNN // LANES * TM, LANES), _F32),
            pltpu.VMEM((SUBLANES, D_RNN), _F32),
        ],
        compiler_params=pltpu.CompilerParams(
            dimension_semantics=("arbitrary", "arbitrary"), vmem_limit_bytes=VMEM_LIMIT),
    )(h, w_rnn, rnn_conv_w, rnn_conv_b, w_blk, b_a, b_x, lam)


def _merge_out_kernel(o_ref, h_ref, mc_ref, x_ref, wro_ref, wgr_ref, wout_ref, fg_ref,
                      out_ref, m_s, res_s):
    o = o_ref[...]
    h = h_ref[...]
    for j in range(D_MODEL // CW):
        cols = slice(j * CW, (j + 1) * CW)
        yr = _dot(o, wro_ref[:, cols])
        gr = _dot(h, wgr_ref[:, cols])
        m_s[:, cols] = (mc_ref[:, cols] + _sigmoid(gr) * yr).astype(_BF16)
    merged = m_s[...]
    for j in range(D_MODEL // CW):
        cols = slice(j * CW, (j + 1) * CW)
        res_s[:, cols] = x_ref[:, cols] + _dot(merged, wout_ref[:, cols])
    res = res_s[...]
    ms = jnp.mean(res * res, axis=-1, keepdims=True)
    out_ref[...] = res * lax.rsqrt(ms + NORM_EPS) * fg_ref[...]


def _merge_out(o, h, mc, x, w_rnn_out, w_grnn, w_out, final_g):
    m = x.shape[0]
    return pl.pallas_call(
        _merge_out_kernel,
        name="merge_out",
        grid=(m // TM,),
        in_specs=[
            pl.BlockSpec((TM, D_RNN), lambda i: (i, 0)),
            pl.BlockSpec((TM, D_MODEL), lambda i: (i, 0)),
            pl.BlockSpec((TM, D_MODEL), lambda i: (i, 0)),
            pl.BlockSpec((TM, D_MODEL), lambda i: (i, 0)),
            _resident((D_RNN, D_MODEL + W_PAD)),
            _resident((D_MODEL, D_MODEL + W_PAD)),
            _resident((D_MODEL, D_MODEL + W_PAD)),
            _resident((1, D_MODEL)),
        ],
        out_specs=pl.BlockSpec((TM, D_MODEL), lambda i: (i, 0)),
        out_shape=jax.ShapeDtypeStruct((m, D_MODEL), _F32),
        scratch_shapes=[pltpu.VMEM((TM, D_MODEL), _BF16), pltpu.VMEM((TM, D_MODEL), _F32)],
        compiler_params=pltpu.CompilerParams(
            dimension_semantics=("arbitrary",), vmem_limit_bytes=VMEM_LIMIT),
    )(o, h, mc, x, w_rnn_out, w_grnn, w_out, final_g)


def _widened(w):
    return jnp.concatenate([w, w[:, :W_PAD]], axis=1).astype(_BF16)


def _window_gate_weights(w_a, w_x):
    per = RNN_GROUP // RNN_BLOCK_DIM

    def windows(w):
        w = w.reshape(RNN_BLOCKS // per, per, RNN_BLOCK_DIM, RNN_BLOCK_DIM)
        cols = []
        for j in range(per):
            off = (j * RNN_BLOCK_DIM) % LANES
            pad = (off, RG_WIN - RNN_BLOCK_DIM - off)
            cols.append(jnp.pad(w[:, j], ((0, 0), pad, pad)))
        return jnp.stack(cols, axis=1).reshape(RNN_BLOCKS, RG_WIN, RG_WIN)

    return jnp.concatenate([windows(w_a), windows(w_x)], axis=2).astype(_BF16)


def _layer(x, norm_g, w_in, conv_dw_w, conv_dw_b, conv_ln_g, conv_ln_b, w_conv_out,
           rnn_conv_w, rnn_conv_b, w_rg_a, b_rg_a, w_rg_x, b_rg_x, rg_lambda,
           w_rnn_out, w_out, out_g):
    bsz, seq, _ = x.shape
    m = bsz * seq
    row = lambda v: v.reshape(1, -1).astype(_F32)
    bounds = [0]
    for width in (D_CONV, D_CONV, D_CONV, D_RNN, D_RNN, D_MODEL, D_MODEL):
        bounds.append(bounds[-1] + width)
    seg = [w_in[:, bounds[i]:bounds[i + 1]] for i in range(7)]
    _, _, _, w_rx, w_rgate, _, w_grnn = seg
    wide = lambda i: w_in[:, bounds[i]:bounds[i + 1] + W_PAD].astype(_BF16)
    w_val, w_glu, w_cgate, w_gconv = wide(0), wide(1), wide(2), wide(5)
    group = lambda w, g: w[:, g * RNN_GROUP:(g + 1) * RNN_GROUP]
    w_rnn = jnp.concatenate([group(w, g) for g in range(D_RNN // RNN_GROUP) for w in (w_rx, w_rgate)]
                            + [w_rx[:, :W_PAD]], axis=1).astype(_BF16)

    h, co = _conv_front(x, row(norm_g), w_val, w_glu, conv_dw_w.astype(_F32), row(conv_dw_b))
    mc = _conv_back(co.reshape(m, D_CONV), h.reshape(m, D_MODEL), row(conv_ln_g), row(conv_ln_b),
                    w_cgate, _widened(w_conv_out), w_gconv)
    o = _rnn_front(h, w_rnn, rnn_conv_w.astype(_F32), row(rnn_conv_b),
                   _window_gate_weights(w_rg_a, w_rg_x), row(b_rg_a), row(b_rg_x), row(rg_lambda))
    out = _merge_out(o.reshape(m, D_RNN), h.reshape(m, D_MODEL), mc, x.reshape(m, D_MODEL),
                     _widened(w_rnn_out), _widened(w_grnn), _widened(w_out), row(out_g))
    return out.reshape(bsz, seq, D_MODEL)


def kernel(x, norm_g, w_in, conv_dw_w, conv_dw_b, conv_ln_g, conv_ln_b, w_conv_out, rnn_conv_w,
           rnn_conv_b, w_rg_a, b_rg_a, w_rg_x, b_rg_x, rg_lambda, w_rnn_out, w_out, final_norm_g):
    depth = norm_g.shape[0]
    assert depth == 1, "the final RMSNorm is fused into the single layer's last call"
    return _layer(x, norm_g[0], w_in[0], conv_dw_w[0], conv_dw_b[0], conv_ln_g[0], conv_ln_b[0],
                  w_conv_out[0], rnn_conv_w[0], rnn_conv_b[0], w_rg_a[0], b_rg_a[0], w_rg_x[0],
                  b_rg_x[0], rg_lambda[0], w_rnn_out[0], w_out[0], final_norm_g)
```
